```python
import math
import jax, jax.numpy as jnp
from jax import lax
import numpy as np

D_MODEL = 2048
BATCH = 4
SEQ = 4096
DEPTH = 1
DEC_BATCH = 16
DEC_SEQ = 2048
PAST_LEN = 128

DA_HEADS = D_MODEL // 512
DA_HEAD_DIM = 128
DA_VDIM = 2 * DA_HEAD_DIM
DA_WIDTH = DA_HEADS * DA_VDIM
NA_HEADS = D_MODEL // 256
NA_HEAD_DIM = 128
NA_WIDTH = NA_HEADS * NA_HEAD_DIM
GRID_W = 64
NA_ROWS = 8
NA_COLS = 16
Q_BLOCK = 128
D_FF = 4 * D_MODEL
ALIBI_MAX_BIAS = 8.0
EPS = 1e-6
IN_W = 3 * DA_WIDTH + 3 * NA_WIDTH + 2 * D_MODEL

kernel_name = "hybrid_diffattn_natten_encoder"


def lambda_init(layer_idx):
    return 0.8 - 0.6 * math.exp(-0.3 * layer_idx)


def rmsnorm(x, g):
    xf = x.astype(jnp.float32)
    y = xf * lax.rsqrt(jnp.mean(xf * xf, axis=-1, keepdims=True) + EPS)
    return (y * g.astype(jnp.float32)).astype(x.dtype)


def diff_attention(q, k, v, lam):
    B, S = q.shape[0], q.shape[1]
    nb = S // Q_BLOCK
    qb = jnp.moveaxis(q.reshape(B, nb, Q_BLOCK, DA_HEADS, 2, DA_HEAD_DIM), 1, 0)
    slopes = jnp.exp2(-ALIBI_MAX_BIAS * jnp.arange(1, DA_HEADS + 1, dtype=jnp.float32) / DA_HEADS)
    key_pos = jnp.arange(S, dtype=jnp.float32)
    scale = DA_HEAD_DIM ** -0.5

    def block(args):
        qi, i = args
        qpos = (i * Q_BLOCK + jnp.arange(Q_BLOCK)).astype(jnp.float32)
        s = jnp.einsum('bqhcd,bkhcd->bhcqk', qi, k).astype(jnp.float32) * scale
        dist = jnp.abs(qpos[:, None] - key_pos[None, :])
        s = s - slopes[None, :, None, None, None] * dist[None, None, None]
        p = jax.nn.softmax(s, axis=-1)
        a = p[:, :, 0] - lam * p[:, :, 1]
        return jnp.einsum('bhqk,bkhe->bqhe', a.astype(v.dtype), v)

    out = lax.map(block, (qb, jnp.arange(nb)))
    return jnp.moveaxis(out, 0, 1).reshape(B, S, DA_HEADS, DA_VDIM)


def neighborhood_attention(q, k, v, rpb):
    B, S = q.shape[0], q.shape[1]
    R = S // GRID_W
    KH = min(NA_ROWS, R)
    qg = q.reshape(B, R, GRID_W, NA_HEADS, NA_HEAD_DIM)
    kg = k.reshape(B, R, GRID_W, NA_HEADS, NA_HEAD_DIM)
    vg = v.reshape(B, R, GRID_W, NA_HEADS, NA_HEAD_DIM)
    rows = jnp.arange(R)
    row_start = jnp.clip(rows - KH // 2, 0, R - KH)
    cols = jnp.arange(GRID_W)
    col_start = jnp.clip(cols - NA_COLS // 2, 0, GRID_W - NA_COLS)
    col_idx = col_start[:, None] + jnp.arange(NA_COLS)[None, :]
    dc = col_idx - cols[:, None] + (NA_COLS - 1)
    scale = NA_HEAD_DIM ** -0.5

    def row(args):
        qr, r, rs = args
        kr = lax.dynamic_slice_in_dim(kg, rs, KH, axis=1)
        vr = lax.dynamic_slice_in_dim(vg, rs, KH, axis=1)
        kw = jnp.take(kr, col_idx, axis=2)
        vw = jnp.take(vr, col_idx, axis=2)
        s = jnp.einsum('bchd,brcjhd->bhcrj', qr, kw).astype(jnp.float32) * scale
        dr = rs + jnp.arange(KH) - r + (NA_ROWS - 1)
        bias = rpb[:, dr[None, :, None], dc[:, None, :]].astype(jnp.float32)
        s = s + bias[None]
        p = jax.nn.softmax(s.reshape(B, NA_HEADS, GRID_W, KH * NA_COLS), axis=-1)
        p = p.reshape(B, NA_HEADS, GRID_W, KH, NA_COLS).astype(v.dtype)
        return jnp.einsum('bhcrj,brcjhd->bchd', p, vw)

    out = lax.map(row, (jnp.moveaxis(qg, 1, 0), rows, row_start))
    return jnp.moveaxis(out, 0, 1).reshape(B, S, NA_WIDTH)


def encoder_layer(l, x, c, w_ada, b_ada, g_mix, w_in, lam_q1, lam_k1, lam_q2, lam_k2,
                  g_subln, rpb, w_pa, w_pb, w_out, g_mlp, w1, w2):
    B, S, _ = x.shape
    ada = (c @ w_ada + b_ada)[:, None, :]
    sh1, sc1, gt1, sh2, sc2, gt2 = jnp.split(ada, 6, axis=-1)

    h = rmsnorm(x, g_mix) * (1 + sc1) + sh1
    proj = h @ w_in
    splits = np.cumsum([DA_WIDTH, DA_WIDTH, DA_WIDTH, NA_WIDTH, NA_WIDTH, NA_WIDTH, D_MODEL])
    qa, ka, va, qn, kn, vn, ga, gb = jnp.split(proj, [int(s_) for s_ in splits], axis=-1)

    lam_f = lambda_init(l)
    lam = (jnp.exp(jnp.sum(lam_q1.astype(jnp.float32) * lam_k1.astype(jnp.float32)))
           - jnp.exp(jnp.sum(lam_q2.astype(jnp.float32) * lam_k2.astype(jnp.float32))) + lam_f)
    oa = diff_attention(qa.reshape(B, S, DA_HEADS, 2, DA_HEAD_DIM),
                        ka.reshape(B, S, DA_HEADS, 2, DA_HEAD_DIM),
                        va.reshape(B, S, DA_HEADS, DA_VDIM), lam)
    oa = (rmsnorm(oa, g_subln) * (1.0 - lam_f)).reshape(B, S, DA_WIDTH)

    on = neighborhood_attention(qn.reshape(B, S, NA_HEADS, NA_HEAD_DIM),
                                kn.reshape(B, S, NA_HEADS, NA_HEAD_DIM),
                                vn.reshape(B, S, NA_HEADS, NA_HEAD_DIM), rpb)

    merged = jax.nn.sigmoid(ga) * (oa @ w_pa) + jax.nn.sigmoid(gb) * (on @ w_pb)
    x = x + gt1 * (merged @ w_out)

    h2 = rmsnorm(x, g_mlp) * (1 + sc2) + sh2
    x = x + gt2 * (jnp.square(jax.nn.relu(h2 @ w1)) @ w2)
    return x


def run_trunk(x, c, w_ada, b_ada, g_mix, w_in, lam_q1, lam_k1, lam_q2, lam_k2,
              g_subln, rpb, w_pa, w_pb, w_out, g_mlp, w1, w2, g_final):
    for l in range(DEPTH):
        x = encoder_layer(l, x, c, w_ada[l], b_ada[l], g_mix[l], w_in[l], lam_q1[l], lam_k1[l],
                          lam_q2[l], lam_k2[l], g_subln[l], rpb[l], w_pa[l], w_pb[l], w_out[l],
                          g_mlp[l], w1[l], w2[l])
    return rmsnorm(x, g_final)


def setup_inputs(seed: int = 0) -> dict:
    key = jax.random.key(seed)
    ks = jax.random.split(key, 24)
    f32 = jnp.float32

    def nrm(k, shape, s):
        return jax.random.normal(k, shape, f32) * s

    return {
        "x_prompt": nrm(ks[0], (BATCH, SEQ, D_MODEL), 1.0),
        "x_sample": nrm(ks[1], (DEC_BATCH, DEC_SEQ, D_MODEL), 1.0),
        "c_prompt": nrm(ks[2], (BATCH, D_MODEL), 1.0),
        "c_sample": nrm(ks[3], (DEC_BATCH, D_MODEL), 1.0),
        "w_ada": nrm(ks[4], (DEPTH, D_MODEL, 6 * D_MODEL), 0.5 * D_MODEL ** -0.5),
        "b_ada": nrm(ks[5], (DEPTH, 6 * D_MODEL), 0.02),
        "g_mix": 1.0 + nrm(ks[6], (DEPTH, D_MODEL), 0.02),
        "w_in": nrm(ks[7], (DEPTH, D_MODEL, IN_W), D_MODEL ** -0.5),
        "lam_q1": nrm(ks[8], (DEPTH, DA_HEAD_DIM), 0.1),
        "lam_k1": nrm(ks[9], (DEPTH, DA_HEAD_DIM), 0.1),
        "lam_q2": nrm(ks[10], (DEPTH, DA_HEAD_DIM), 0.1),
        "lam_k2": nrm(ks[11], (DEPTH, DA_HEAD_DIM), 0.1),
        "g_subln": 1.0 + nrm(ks[12], (DEPTH, DA_VDIM), 0.02),
        "rpb": nrm(ks[13], (DEPTH, NA_HEADS, 2 * NA_ROWS - 1, 2 * NA_COLS - 1), 0.1),
        "w_pa": nrm(ks[14], (DEPTH, DA_WIDTH, D_MODEL), DA_WIDTH ** -0.5),
        "w_pb": nrm(ks[15], (DEPTH, NA_WIDTH, D_MODEL), NA_WIDTH ** -0.5),
        "w_out": nrm(ks[16], (DEPTH, D_MODEL, D_MODEL), D_MODEL ** -0.5),
        "g_mlp": 1.0 + nrm(ks[17], (DEPTH, D_MODEL), 0.02),
        "w1": nrm(ks[18], (DEPTH, D_MODEL, D_FF), D_MODEL ** -0.5),
        "w2": nrm(ks[19], (DEPTH, D_FF, D_MODEL), D_FF ** -0.5),
        "g_final": 1.0 + nrm(ks[20], (D_MODEL,), 0.02),
    }


def reference(x_prompt, x_sample, c_prompt, c_sample, w_ada, b_ada, g_mix, w_in,
              lam_q1, lam_k1, lam_q2, lam_k2, g_subln, rpb, w_pa, w_pb, w_out,
              g_mlp, w1, w2, g_final):
    y_prompt = run_trunk(x_prompt, c_prompt, w_ada, b_ada, g_mix, w_in, lam_q1, lam_k1,
                         lam_q2, lam_k2, g_subln, rpb, w_pa, w_pb, w_out, g_mlp, w1, w2, g_final)
    y_sample = run_trunk(x_sample, c_sample, w_ada, b_ada, g_mix, w_in, lam_q1, lam_k1,
                         lam_q2, lam_k2, g_subln, rpb, w_pa, w_pb, w_out, g_mlp, w1, w2, g_final)
    return (y_prompt, y_sample)
```

```python
import functools
import math

import jax
import jax.numpy as jnp
from jax import lax
from jax.experimental import pallas as pl
from jax.experimental.pallas import tpu as pltpu

F32 = jnp.float32
BF16 = jnp.bfloat16

D_MODEL = 2048
DA_HEADS = 4
DA_HEAD_DIM = 128
DA_VDIM = 256
DA_WIDTH = DA_HEADS * DA_VDIM
NA_HEADS = 8
NA_HEAD_DIM = 128
NA_WIDTH = NA_HEADS * NA_HEAD_DIM
GRID_W = 64
NA_ROWS = 8
NA_COLS = 16
D_FF = 4 * D_MODEL
ALIBI_MAX_BIAS = 8.0
EPS = 1e-6
IN_W = 3 * DA_WIDTH + 3 * NA_WIDTH + 2 * D_MODEL
LAMBDA_INIT = 0.8 - 0.6 * math.exp(-0.3 * 0)

COL_QA, COL_KA, COL_VA = 0, DA_WIDTH, 2 * DA_WIDTH
COL_QN = 3 * DA_WIDTH
COL_KN = COL_QN + NA_WIDTH
COL_VN = COL_KN + NA_WIDTH
COL_GA = COL_VN + NA_WIDTH
COL_GB = COL_GA + D_MODEL

NA_GROUP = 4
NA_KROWS = NA_GROUP + NA_ROWS
NA_Q = NA_GROUP * GRID_W
NA_K = NA_KROWS * GRID_W
NA_RPB_R = 2 * NA_ROWS - 1
NA_RPB_C = 2 * NA_COLS - 1
NEG = -1e30

VMEM_LIMIT = 56 * 1024 * 1024


def _params(sem):
    return pltpu.CompilerParams(dimension_semantics=sem, vmem_limit_bytes=VMEM_LIMIT)


def _ada_kernel(c_ref, w_ref, b_ref, o_ref):
    c = c_ref[...].astype(BF16)
    w = w_ref[...].astype(BF16)
    o_ref[...] = jnp.dot(c, w, preferred_element_type=F32) + b_ref[...]


def _ada(c_all, w_ada, b_ada, tn=1024):
    nb = c_all.shape[0]
    n = w_ada.shape[1]
    return pl.pallas_call(
        _ada_kernel,
        out_shape=jax.ShapeDtypeStruct((nb, n), F32),
        grid=(n // tn,),
        in_specs=[
            pl.BlockSpec((nb, D_MODEL), lambda j: (0, 0)),
            pl.BlockSpec((D_MODEL, tn), lambda j: (0, j)),
            pl.BlockSpec((1, tn), lambda j: (0, j)),
        ],
        out_specs=pl.BlockSpec((nb, tn), lambda j: (0, j)),
        compiler_params=_params(("parallel",)),
        name="ada_proj",
    )(c_all, w_ada, b_ada.reshape(1, n))


def _inproj_kernel(x_ref, g_ref, sc_ref, sh_ref, w_ref, o_ref, h_ref, *, rows):
    @pl.when(pl.program_id(2) == 0)
    def _():
        g = g_ref[...]
        sc = 1.0 + sc_ref[0]
        sh = sh_ref[0]

        def body(r, carry):
            r0 = pl.multiple_of(r * rows, rows)
            x = x_ref[0, pl.ds(r0, rows), :]
            ms = jnp.mean(x * x, axis=-1, keepdims=True)
            y = x * lax.rsqrt(ms + EPS) * g
            h_ref[pl.ds(r0, rows), :] = (y * sc + sh).astype(BF16)
            return carry

        lax.fori_loop(0, x_ref.shape[1] // rows, body, 0)

    o_ref[0] = jnp.dot(h_ref[...], w_ref[...], preferred_element_type=F32).astype(BF16)


def _inproj(x, ada3, g_mix, w_in, tm=1024, tn=1024):
    b, s, _ = x.shape
    return pl.pallas_call(
        functools.partial(_inproj_kernel, rows=128),
        out_shape=jax.ShapeDtypeStruct((b, s, IN_W), BF16),
        grid=(b, s // tm, IN_W // tn),
        in_specs=[
            pl.BlockSpec((1, tm, D_MODEL), lambda bi, i, n: (bi, i, 0)),
            pl.BlockSpec((1, D_MODEL), lambda bi, i, n: (0, 0)),
            pl.BlockSpec((1, 1, D_MODEL), lambda bi, i, n: (bi * 6 + 1, 0, 0)),
            pl.BlockSpec((1, 1, D_MODEL), lambda bi, i, n: (bi * 6 + 0, 0, 0)),
            pl.BlockSpec((D_MODEL, tn), lambda bi, i, n: (0, n)),
        ],
        out_specs=pl.BlockSpec((1, tm, tn), lambda bi, i, n: (bi, i, n)),
        scratch_shapes=[pltpu.VMEM((tm, D_MODEL), BF16)],
        compiler_params=_params(("parallel", "parallel", "arbitrary")),
        name="in_proj",
    )(x, g_mix, ada3, ada3, w_in)


def _da_kernel(slope_ref, lam_ref, q_ref, k_ref, v_ref, g_ref, o_ref, acc_ref, *, tk):
    h = pl.program_id(1)
    i = pl.program_id(2)
    tq = q_ref.shape[1]
    s_len = k_ref.shape[1]
    slope = slope_ref[h]
    scale = DA_HEAD_DIM ** -0.5

    lq = lam_ref[...]
    e1 = jnp.exp(jnp.sum(lq[0:1] * lq[1:2], axis=-1, keepdims=True))
    e2 = jnp.exp(jnp.sum(lq[2:3] * lq[3:4], axis=-1, keepdims=True))
    lam = e1 - e2 + LAMBDA_INIT

    base = (lax.broadcasted_iota(jnp.int32, (tq, tk), 0)
            - lax.broadcasted_iota(jnp.int32, (tq, tk), 1)) + i * tq
    q = q_ref[0]
    acc_ref[...] = jnp.zeros_like(acc_ref)

    def body(j, carry):
        j0 = pl.multiple_of(j * tk, tk)
        kc = k_ref[0, pl.ds(j0, tk), :]
        vc = v_ref[0, pl.ds(j0, tk), :]
        bias = jnp.abs(base - j * tk).astype(F32) * slope
        out = []
        for c in range(2):
            m, l = carry[2 * c], carry[2 * c + 1]
            s = lax.dot_general(q[:, c * DA_HEAD_DIM:(c + 1) * DA_HEAD_DIM],
                                kc[:, c * DA_HEAD_DIM:(c + 1) * DA_HEAD_DIM],
                                (((1,), (1,)), ((), ())), preferred_element_type=F32)
            s = s * scale - bias
            m_new = jnp.maximum(m, jnp.max(s, axis=-1, keepdims=True))
            alpha = jnp.exp(m - m_new)
            p = jnp.exp(s - m_new)
            l_new = alpha * l + jnp.sum(p, axis=-1, keepdims=True)
            acc_ref[c] = alpha * acc_ref[c] + jnp.dot(p.astype(BF16), vc, preferred_element_type=F32)
            out += [m_new, l_new]
        return tuple(out)

    m_init = jnp.full((tq, 1), NEG, F32)
    l_init = jnp.zeros((tq, 1), F32)
    m0, l0, m1, l1 = lax.fori_loop(0, s_len // tk, body, (m_init, l_init, m_init, l_init))

    o = acc_ref[0] / l0 - lam * (acc_ref[1] / l1)
    ms = jnp.mean(o * o, axis=-1, keepdims=True)
    y = (o * lax.rsqrt(ms + EPS) * g_ref[...]) * (1.0 - LAMBDA_INIT)
    o_ref[0] = y.astype(BF16)


def _diff_attention(proj, slopes, lam4, g_subln, tq=512, tk=512):
    b, s, _ = proj.shape
    qb, kb, vb = COL_QA // DA_VDIM, COL_KA // DA_VDIM, COL_VA // DA_VDIM
    return pl.pallas_call(
        functools.partial(_da_kernel, tk=tk),
        out_shape=jax.ShapeDtypeStruct((b, s, DA_WIDTH), BF16),
        grid=(b, DA_HEADS, s // tq),
        in_specs=[
            pl.BlockSpec(memory_space=pltpu.SMEM),
            pl.BlockSpec((4, DA_HEAD_DIM), lambda bi, h, i: (0, 0)),
            pl.BlockSpec((1, tq, DA_VDIM), lambda bi, h, i: (bi, i, qb + h)),
            pl.BlockSpec((1, s, DA_VDIM), lambda bi, h, i: (bi, 0, kb + h)),
            pl.BlockSpec((1, s, DA_VDIM), lambda bi, h, i: (bi, 0, vb + h)),
            pl.BlockSpec((1, DA_VDIM), lambda bi, h, i: (0, 0)),
        ],
        out_specs=pl.BlockSpec((1, tq, DA_VDIM), lambda bi, h, i: (bi, i, h)),
        scratch_shapes=[pltpu.VMEM((2, tq, DA_VDIM), F32)],
        compiler_params=_params(("parallel", "parallel", "arbitrary")),
        name="diff_attn",
    )(slopes, lam4, proj, proj, proj, g_subln)


def _na_variant_rows(variant, ri, kr):
    if variant == 0:
        valid, dr = 0 <= kr < NA_ROWS, kr - ri + NA_ROWS - 1
    elif variant == 1:
        valid, dr = ri <= kr < ri + NA_ROWS, kr - ri + NA_ROWS - 1 - NA_GROUP
    else:
        valid, dr = NA_GROUP <= kr < NA_KROWS, kr - ri - 1
    return dr if valid else None


def _na_bias_kernel(rpb_ref, tab_ref, tt_ref):
    h = pl.program_id(0)
    shape = (GRID_W, 2 * GRID_W)
    lane = lax.broadcasted_iota(jnp.int32, shape, 1)
    col = lax.broadcasted_iota(jnp.int32, shape, 0)
    kcol = lane & (GRID_W - 1)
    rel = kcol - col + (NA_COLS - 1)
    cstart = jnp.clip(col - NA_COLS // 2, 0, GRID_W - NA_COLS)
    band = (kcol >= cstart) & (kcol < cstart + NA_COLS)
    neg = jnp.full(shape, NEG, F32)
    for dr in range(NA_RPB_R):
        t = neg
        for dc in range(NA_RPB_C):
            val = rpb_ref[h * (NA_RPB_R * NA_RPB_C) + dr * NA_RPB_C + dc]
            t = jnp.where(rel == dc, val, t)
        tt_ref[dr] = jnp.where(band, t, neg)
    first_half = lane < GRID_W
    for variant in range(3):
        for ri in range(NA_GROUP):
            for pair in range(NA_KROWS // 2):
                da = _na_variant_rows(variant, ri, 2 * pair)
                db = _na_variant_rows(variant, ri, 2 * pair + 1)
                ta = neg if da is None else tt_ref[da]
                tb = neg if db is None else tt_ref[db]
                tab_ref[0, variant, ri * GRID_W:(ri + 1) * GRID_W,
                        pair * 2 * GRID_W:(pair + 1) * 2 * GRID_W] = jnp.where(first_half, ta, tb)


def _na_bias(rpb_flat):
    return pl.pallas_call(
        _na_bias_kernel,
        out_shape=jax.ShapeDtypeStruct((NA_HEADS, 3, NA_Q, NA_K), F32),
        grid=(NA_HEADS,),
        in_specs=[pl.BlockSpec(memory_space=pltpu.SMEM)],
        out_specs=pl.BlockSpec((1, 3, NA_Q, NA_K), lambda h: (h, 0, 0, 0)),
        scratch_shapes=[pltpu.VMEM((NA_RPB_R, GRID_W, 2 * GRID_W), F32)],
        compiler_params=_params(("arbitrary",)),
        name="na_bias",
    )(rpb_flat)


def _na_kernel(q_ref, k_ref, v_ref, tab_ref, o_ref, *, n_rows):
    g = pl.program_id(2)
    ks = jnp.clip(NA_GROUP * g - NA_ROWS // 2, 0, n_rows - NA_KROWS)
    start = pl.multiple_of(ks * GRID_W, GRID_W)
    kw = k_ref[0, pl.ds(start, NA_K), :]
    vw = v_ref[0, pl.ds(start, NA_K), :]
    s = lax.dot_general(q_ref[0], kw, (((1,), (1,)), ((), ())), preferred_element_type=F32)
    s = s * (NA_HEAD_DIM ** -0.5) + tab_ref[0, 0]
    m = jnp.max(s, axis=-1, keepdims=True)
    p = jnp.exp(s - m)
    l = jnp.sum(p, axis=-1, keepdims=True)
    o = jnp.dot(p.astype(BF16), vw, preferred_element_type=F32)
    o_ref[0] = (o / l).astype(BF16)


def _neighborhood_attention(proj, table):
    b, s, _ = proj.shape
    n_rows = s // GRID_W
    assert n_rows % NA_GROUP == 0 and n_rows >= NA_KROWS and NA_ROWS // 2 == NA_GROUP
    n_groups = n_rows // NA_GROUP
    qb, kb, vb = COL_QN // NA_HEAD_DIM, COL_KN // NA_HEAD_DIM, COL_VN // NA_HEAD_DIM

    def variant(g):
        return jnp.where(g == 0, 0, jnp.where(g == n_groups - 1, 2, 1))

    return pl.pallas_call(
        functools.partial(_na_kernel, n_rows=n_rows),
        out_shape=jax.ShapeDtypeStruct((b, s, NA_WIDTH), BF16),
        grid=(b, NA_HEADS, n_groups),
        in_specs=[
            pl.BlockSpec((1, NA_Q, NA_HEAD_DIM), lambda bi, h, g: (bi, g, qb + h)),
            pl.BlockSpec((1, s, NA_HEAD_DIM), lambda bi, h, g: (bi, 0, kb + h)),
            pl.BlockSpec((1, s, NA_HEAD_DIM), lambda bi, h, g: (bi, 0, vb + h)),
            pl.BlockSpec((1, 1, NA_Q, NA_K), lambda bi, h, g: (h, variant(g), 0, 0)),
        ],
        out_specs=pl.BlockSpec((1, NA_Q, NA_HEAD_DIM), lambda bi, h, g: (bi, g, h)),
        compiler_params=_params(("parallel", "parallel", "arbitrary")),
        name="nbr_attn",
    )(proj, proj, proj, table)


def _sigmoid(x):
    return 1.0 / (1.0 + jnp.exp(-x))


def _merge_kernel(x_ref, oa_ref, on_ref, ga_ref, gb_ref, wpa_ref, wpb_ref, wout_ref,
                  gt1_ref, gmlp_ref, sc2_ref, sh2_ref, x1_ref, h2_ref):
    pa = jnp.dot(oa_ref[0], wpa_ref[...], preferred_element_type=F32)
    pb = jnp.dot(on_ref[0], wpb_ref[...], preferred_element_type=F32)
    merged = _sigmoid(ga_ref[0].astype(F32)) * pa + _sigmoid(gb_ref[0].astype(F32)) * pb
    out = jnp.dot(merged.astype(BF16), wout_ref[...], preferred_element_type=F32)
    x1 = x_ref[0] + gt1_ref[0] * out
    x1_ref[0] = x1
    ms = jnp.mean(x1 * x1, axis=-1, keepdims=True)
    y = x1 * lax.rsqrt(ms + EPS) * gmlp_ref[...]
    h2_ref[0] = (y * (1.0 + sc2_ref[0]) + sh2_ref[0]).astype(BF16)


def _merge(x, oa, on, proj, w_pa, w_pb, w_out, ada3, g_mlp, tm=512):
    b, s, _ = x.shape
    ga_b, gb_b = COL_GA // D_MODEL, COL_GB // D_MODEL
    once = pl.Buffered(1)

    def ada_spec(idx):
        return pl.BlockSpec((1, 1, D_MODEL), lambda bi, i: (bi * 6 + idx, 0, 0))

    return pl.pallas_call(
        _merge_kernel,
        out_shape=(jax.ShapeDtypeStruct((b, s, D_MODEL), F32),
                   jax.ShapeDtypeStruct((b, s, D_MODEL), BF16)),
        grid=(b, s // tm),
        in_specs=[
            pl.BlockSpec((1, tm, D_MODEL), lambda bi, i: (bi, i, 0)),
            pl.BlockSpec((1, tm, DA_WIDTH), lambda bi, i: (bi, i, 0)),
            pl.BlockSpec((1, tm, NA_WIDTH), lambda bi, i: (bi, i, 0)),
            pl.BlockSpec((1, tm, D_MODEL), lambda bi, i: (bi, i, ga_b)),
            pl.BlockSpec((1, tm, D_MODEL), lambda bi, i: (bi, i, gb_b)),
            pl.BlockSpec((DA_WIDTH, D_MODEL), lambda bi, i: (0, 0), pipeline_mode=once),
            pl.BlockSpec((NA_WIDTH, D_MODEL), lambda bi, i: (0, 0), pipeline_mode=once),
            pl.BlockSpec((D_MODEL, D_MODEL), lambda bi, i: (0, 0), pipeline_mode=once),
            ada_spec(2),
            pl.BlockSpec((1, D_MODEL), lambda bi, i: (0, 0)),
            ada_spec(4),
            ada_spec(3),
        ],
        out_specs=(pl.BlockSpec((1, tm, D_MODEL), lambda bi, i: (bi, i, 0)),
                   pl.BlockSpec((1, tm, D_MODEL), lambda bi, i: (bi, i, 0))),
        compiler_params=_params(("parallel", "parallel")),
        name="merge_out_proj",
    )(x, oa, on, proj, proj, w_pa, w_pb, w_out, ada3, g_mlp, ada3, ada3)


def _mlp_kernel(h2_ref, w1_ref, w2_ref, x1_ref, gt2_ref, gf_ref, y_ref, acc_ref):
    f = pl.program_id(2)
    u = jnp.dot(h2_ref[0], w1_ref[...], preferred_element_type=F32)
    u = jnp.square(jnp.maximum(u, 0.0)).astype(BF16)
    contrib = jnp.dot(u, w2_ref[...], preferred_element_type=F32)

    @pl.when(f == 0)
    def _():
        acc_ref[...] = contrib

    @pl.when(f != 0)
    def _():
        acc_ref[...] += contrib

    @pl.when(f == pl.num_programs(2) - 1)
    def _():
        x2 = x1_ref[0] + gt2_ref[0] * acc_ref[...]
        ms = jnp.mean(x2 * x2, axis=-1, keepdims=True)
        y_ref[0] = x2 * lax.rsqrt(ms + EPS) * gf_ref[...]


def _mlp(h2, x1, w1, w2, ada3, g_final, tm=512, tf=1024):
    b, s, _ = x1.shape
    return pl.pallas_call(
        _mlp_kernel,
        out_shape=jax.ShapeDtypeStruct((b, s, D_MODEL), F32),
        grid=(b, s // tm, D_FF // tf),
        in_specs=[
            pl.BlockSpec((1, tm, D_MODEL), lambda bi, i, f: (bi, i, 0)),
            pl.BlockSpec((D_MODEL, tf), lambda bi, i, f: (0, f)),
            pl.BlockSpec((tf, D_MODEL), lambda bi, i, f: (f, 0)),
            pl.BlockSpec((1, tm, D_MODEL), lambda bi, i, f: (bi, i, 0)),
            pl.BlockSpec((1, 1, D_MODEL), lambda bi, i, f: (bi * 6 + 5, 0, 0)),
            pl.BlockSpec((1, D_MODEL), lambda bi, i, f: (0, 0)),
        ],
        out_specs=pl.BlockSpec((1, tm, D_MODEL), lambda bi, i, f: (bi, i, 0)),
        scratch_shapes=[pltpu.VMEM((tm, D_MODEL), F32)],
        compiler_params=_params(("parallel", "parallel", "arbitrary")),
        name="mlp_final_norm",
    )(h2, w1, w2, x1, ada3, g_final)


def _trunk(x, ada, w, table, slopes, lam4):
    b = x.shape[0]
    ada3 = ada.reshape(b * 6, 1, D_MODEL)
    proj = _inproj(x, ada3, w["g_mix"], w["w_in"])
    oa = _diff_attention(proj, slopes, lam4, w["g_subln"])
    on = _neighborhood_attention(proj, table)
    x1, h2 = _merge(x, oa, on, proj, w["w_pa"], w["w_pb"], w["w_out"], ada3, w["g_mlp"])
    return _mlp(h2, x1, w["w1"], w["w2"], ada3, w["g_final"])


def kernel(x_prompt, x_sample, c_prompt, c_sample, w_ada, b_ada, g_mix, w_in, lam_q1, lam_k1, lam_q2, lam_k2, g_subln, rpb, w_pa, w_pb, w_out, g_mlp, w1, w2, g_final):
    w = {
        "g_mix": g_mix[0].reshape(1, D_MODEL),
        "w_in": w_in[0].astype(BF16),
        "g_subln": g_subln[0].reshape(1, DA_VDIM),
        "w_pa": w_pa[0].astype(BF16),
        "w_pb": w_pb[0].astype(BF16),
        "w_out": w_out[0].astype(BF16),
        "g_mlp": g_mlp[0].reshape(1, D_MODEL),
        "w1": w1[0].astype(BF16),
        "w2": w2[0].astype(BF16),
        "g_final": g_final.reshape(1, D_MODEL),
    }
    nb_p = c_prompt.shape[0]
    ada = _ada(jnp.concatenate([c_prompt, c_sample], axis=0), w_ada[0], b_ada[0])
    table = _na_bias(rpb[0].reshape(-1))
    slopes = jnp.exp2(-ALIBI_MAX_BIAS * jnp.arange(1, DA_HEADS + 1, dtype=F32) / DA_HEADS)
    lam4 = jnp.stack([lam_q1[0], lam_k1[0], lam_q2[0], lam_k2[0]], axis=0)
    y_prompt = _trunk(x_prompt, ada[:nb_p], w, table, slopes, lam4)
    y_sample = _trunk(x_sample, ada[nb_p:], w, table, slopes, lam4)
    return (y_prompt, y_sample)
```

```python
import functools
import math

import jax
import jax.numpy as jnp
import ml_dtypes
import numpy as np
from jax import lax
from jax.experimental import pallas as pl
from jax.experimental.pallas import tpu as pltpu

F32 = jnp.float32
BF16 = jnp.bfloat16

D_MODEL = 2048
DA_HEADS = 4
DA_HEAD_DIM = 128
DA_VDIM = 256
DA_WIDTH = DA_HEADS * DA_VDIM
NA_HEADS = 8
NA_HEAD_DIM = 128
NA_WIDTH = NA_HEADS * NA_HEAD_DIM
GRID_W = 64
NA_ROWS = 8
NA_COLS = 16
D_FF = 4 * D_MODEL
ALIBI_MAX_BIAS = 8.0
EPS = 1e-6
IN_W = 3 * DA_WIDTH + 3 * NA_WIDTH + 2 * D_MODEL
LAMBDA_INIT = 0.8 - 0.6 * math.exp(-0.3 * 0)

COL_QA, COL_KA, COL_VA = 0, DA_WIDTH, 2 * DA_WIDTH
COL_QN = 3 * DA_WIDTH
COL_KN = COL_QN + NA_WIDTH
COL_VN = COL_KN + NA_WIDTH
COL_GA = COL_VN + NA_WIDTH
COL_GB = COL_GA + D_MODEL

NA_GROUP = 4
NA_KROWS = NA_GROUP + NA_ROWS
NA_Q = NA_GROUP * GRID_W
NA_K = NA_KROWS * GRID_W
NA_RPB_R = 2 * NA_ROWS - 1
NA_RPB_C = 2 * NA_COLS - 1
NEG = -1e30

VMEM_LIMIT = 56 * 1024 * 1024


def _params(sem):
    return pltpu.CompilerParams(dimension_semantics=sem, vmem_limit_bytes=VMEM_LIMIT)


def _ada_kernel(c_ref, w_ref, b_ref, o_ref):
    c = c_ref[...].astype(BF16)
    w = w_ref[...].astype(BF16)
    o_ref[...] = jnp.dot(c, w, preferred_element_type=F32) + b_ref[...]


def _ada(c_all, w_ada, b_ada, tn=1024):
    nb = c_all.shape[0]
    n = w_ada.shape[1]
    return pl.pallas_call(
        _ada_kernel,
        out_shape=jax.ShapeDtypeStruct((nb, n), F32),
        grid=(n // tn,),
        in_specs=[
            pl.BlockSpec((nb, D_MODEL), lambda j: (0, 0)),
            pl.BlockSpec((D_MODEL, tn), lambda j: (0, j)),
            pl.BlockSpec((1, tn), lambda j: (0, j)),
        ],
        out_specs=pl.BlockSpec((nb, tn), lambda j: (0, j)),
        compiler_params=_params(("parallel",)),
        name="ada_proj",
    )(c_all, w_ada, b_ada.reshape(1, n))


def _inproj_kernel(x_ref, g_ref, sc_ref, sh_ref, w_ref, o_ref, h_ref, *, rows):
    @pl.when(pl.program_id(2) == 0)
    def _():
        g = g_ref[...]
        sc = 1.0 + sc_ref[0]
        sh = sh_ref[0]

        def body(r, carry):
            r0 = pl.multiple_of(r * rows, rows)
            x = x_ref[0, pl.ds(r0, rows), :]
            ms = jnp.mean(x * x, axis=-1, keepdims=True)
            y = x * lax.rsqrt(ms + EPS) * g
            h_ref[pl.ds(r0, rows), :] = (y * sc + sh).astype(BF16)
            return carry

        lax.fori_loop(0, x_ref.shape[1] // rows, body, 0)

    o_ref[0] = jnp.dot(h_ref[...], w_ref[...], preferred_element_type=F32).astype(BF16)


def _inproj(x, ada3, g_mix, w_in, tm=1024, tn=1024):
    b, s, _ = x.shape
    return pl.pallas_call(
        functools.partial(_inproj_kernel, rows=128),
        out_shape=jax.ShapeDtypeStruct((b, s, IN_W), BF16),
        grid=(b, s // tm, IN_W // tn),
        in_specs=[
            pl.BlockSpec((1, tm, D_MODEL), lambda bi, i, n: (bi, i, 0)),
            pl.BlockSpec((1, D_MODEL), lambda bi, i, n: (0, 0)),
            pl.BlockSpec((1, 1, D_MODEL), lambda bi, i, n: (bi * 6 + 1, 0, 0)),
            pl.BlockSpec((1, 1, D_MODEL), lambda bi, i, n: (bi * 6 + 0, 0, 0)),
            pl.BlockSpec((D_MODEL, tn), lambda bi, i, n: (0, n)),
        ],
        out_specs=pl.BlockSpec((1, tm, tn), lambda bi, i, n: (bi, i, n)),
        scratch_shapes=[pltpu.VMEM((tm, D_MODEL), BF16)],
        compiler_params=_params(("parallel", "parallel", "arbitrary")),
        name="in_proj",
    )(x, g_mix, ada3, ada3, w_in)


DA_NCH = 4


def _alibi_lane_constants():
    csum = np.zeros((DA_HEADS,), np.float32)
    aq = np.zeros((DA_HEADS, DA_HEAD_DIM, 1), np.float32)
    ak = np.zeros((DA_HEADS, 1, DA_HEAD_DIM), np.float32)
    for h in range(DA_HEADS):
        slope = 2.0 ** (-ALIBI_MAX_BIAS * (h + 1) / DA_HEADS)
        rem = slope * math.sqrt(DA_HEAD_DIM)
        pieces = []
        for _ in range(DA_NCH):
            p = float(np.float32(rem).astype(ml_dtypes.bfloat16).astype(np.float32))
            pieces.append(p)
            rem -= p
        csum[h] = np.float32(sum(pieces))
        for d, weight in enumerate((256.0, 16.0, 1.0)):
            for n, p in enumerate(pieces):
                ak[h, 0, d * DA_NCH + n] = -weight * p
                aq[h, (3 + d) * DA_NCH + n, 0] = p
    return csum, aq, ak


def _da_kernel(csum_ref, lam_ref, aq_ref, ak_ref, q_ref, k_ref, v_ref, g_ref, o_ref,
               acc_ref, qt_ref, ka_ref, vt_ref, corr_ref, s_ref, *, tk):
    h = pl.program_id(1)
    i = pl.program_id(2)
    tq = q_ref.shape[1]
    s_len = k_ref.shape[1]
    scale = DA_HEAD_DIM ** -0.5
    n3 = 3 * DA_NCH

    lq = lam_ref[...]
    e1 = jnp.exp(jnp.sum(lq[0:1] * lq[1:2], axis=-1, keepdims=True))
    e2 = jnp.exp(jnp.sum(lq[2:3] * lq[3:4], axis=-1, keepdims=True))
    lam = e1 - e2 + LAMBDA_INIT

    @pl.when(i == 0)
    def _():
        ck = ak_ref[0]
        lane = lax.broadcasted_iota(jnp.int32, (tk, DA_HEAD_DIM), 1)
        row = lax.broadcasted_iota(jnp.int32, (tk, DA_HEAD_DIM), 0)

        def build(r, carry):
            r0 = pl.multiple_of(r * tk, tk)
            pos = row + r * tk
            l2 = lane - n3
            dig = jnp.where(l2 < DA_NCH, (pos >> 8) << 8,
                            jnp.where(l2 < 2 * DA_NCH, ((pos >> 4) & 15) << 4, pos & 15)).astype(F32)
            kpos = jnp.where((lane >= n3) & (lane < 2 * n3), dig, ck).astype(BF16)
            for c in range(2):
                ka_ref[c, pl.ds(r0, tk), 0:DA_HEAD_DIM] = k_ref[0, pl.ds(r0, tk), c * DA_HEAD_DIM:(c + 1) * DA_HEAD_DIM]
                ka_ref[c, pl.ds(r0, tk), DA_HEAD_DIM:] = kpos
            vt_ref[r] = v_ref[0, pl.ds(r0, tk), :].astype(F32).T.astype(BF16)
            return carry

        lax.fori_loop(0, s_len // tk, build, 0)
        gap = (lax.broadcasted_iota(jnp.int32, (tk, tq), 0)
               - lax.broadcasted_iota(jnp.int32, (tk, tq), 1))
        corr_ref[...] = jnp.maximum(gap, 0).astype(F32) * (-2.0 * csum_ref[h])

    prow = lax.broadcasted_iota(jnp.int32, (DA_HEAD_DIM, tq), 0)
    pos = lax.broadcasted_iota(jnp.int32, (DA_HEAD_DIM, tq), 1) + i * tq
    dig = jnp.where(prow < DA_NCH, pos >> 8, jnp.where(prow < 2 * DA_NCH, (pos >> 4) & 15, pos & 15)).astype(F32)
    qpos = jnp.where(prow < n3, dig, aq_ref[0])
    for c in range(2):
        qct = q_ref[0, :, c * DA_HEAD_DIM:(c + 1) * DA_HEAD_DIM].astype(F32).T.astype(BF16)
        for sign in range(2):
            qt_ref[sign, c, 0:DA_HEAD_DIM, :] = qct
            qt_ref[sign, c, DA_HEAD_DIM:, :] = (qpos if sign == 0 else -qpos).astype(BF16)
    acc_ref[...] = jnp.zeros_like(acc_ref)

    c2 = scale * math.log2(math.e)

    def scores(j, slot, diag=False):
        j0 = pl.multiple_of(j * tk, tk)
        sign = (j > i).astype(jnp.int32)
        maxes = []
        for c in range(2):
            s = jnp.dot(ka_ref[c, pl.ds(j0, tk), :], qt_ref[sign, c], preferred_element_type=F32)
            if diag:
                s = s + corr_ref[...]
            s_ref[slot, c] = s
            maxes.append(jnp.max(s, axis=0, keepdims=True))
        return tuple(maxes)

    def consume(j, slot, maxes, stats):
        vt = vt_ref[j]
        out = []
        for c in range(2):
            m, l = stats[2 * c], stats[2 * c + 1]
            m_new = jnp.maximum(m, maxes[c])
            alpha = jnp.exp2((m - m_new) * c2)
            p = jnp.exp2((s_ref[slot, c] - m_new) * c2)
            l_new = alpha * l + jnp.sum(p, axis=0, keepdims=True)
            acc_ref[c] = alpha * acc_ref[c] + jnp.dot(vt, p.astype(BF16), preferred_element_type=F32)
            out += [m_new, l_new]
        return tuple(out)

    nk = s_len // tk

    def tile_of(t):
        u = t - 1
        return u + (u >= i).astype(jnp.int32)

    m_init = jnp.full((1, tq), NEG, F32)
    l_init = jnp.zeros((1, tq), F32)
    mx = scores(i, 0, diag=True)

    def pair(p, carry):
        j_a, mx_a, stats = carry
        j_b = tile_of(2 * p + 1)
        mx_b = scores(j_b, 1)
        stats = consume(j_a, 0, mx_a, stats)
        j_c = tile_of(2 * p + 2)
        mx_c = scores(j_c, 0)
        stats = consume(j_b, 1, mx_b, stats)
        return j_c, mx_c, stats

    j_a, mx_a, stats = lax.fori_loop(0, nk // 2 - 1, pair, (i, mx, (m_init, l_init, m_init, l_init)))
    j_b = tile_of(nk - 1)
    mx_b = scores(j_b, 1)
    stats = consume(j_a, 0, mx_a, stats)
    m0, l0, m1, l1 = consume(j_b, 1, mx_b, stats)

    ot = acc_ref[0] / l0 - lam * (acc_ref[1] / l1)
    ms = jnp.mean(ot * ot, axis=0, keepdims=True)
    y = (ot * lax.rsqrt(ms + EPS)).T * g_ref[...] * (1.0 - LAMBDA_INIT)
    o_ref[0] = y.astype(BF16)


def _diff_attention(proj, lam4, g_subln, tq=512):
    b, s, _ = proj.shape
    tk = tq
    assert s % (2 * tk) == 0 and s <= 16 ** 3
    qb, kb, vb = COL_QA // DA_VDIM, COL_KA // DA_VDIM, COL_VA // DA_VDIM
    csum, aq, ak = _alibi_lane_constants()
    return pl.pallas_call(
        functools.partial(_da_kernel, tk=tk),
        out_shape=jax.ShapeDtypeStruct((b, s, DA_WIDTH), BF16),
        grid=(b, DA_HEADS, s // tq),
        in_specs=[
            pl.BlockSpec(memory_space=pltpu.SMEM),
            pl.BlockSpec((4, DA_HEAD_DIM), lambda bi, h, i: (0, 0)),
            pl.BlockSpec((1, DA_HEAD_DIM, 1), lambda bi, h, i: (h, 0, 0)),
            pl.BlockSpec((1, 1, DA_HEAD_DIM), lambda bi, h, i: (h, 0, 0)),
            pl.BlockSpec((1, tq, DA_VDIM), lambda bi, h, i: (bi, i, qb + h)),
            pl.BlockSpec((1, s, DA_VDIM), lambda bi, h, i: (bi, 0, kb + h)),
            pl.BlockSpec((1, s, DA_VDIM), lambda bi, h, i: (bi, 0, vb + h)),
            pl.BlockSpec((1, DA_VDIM), lambda bi, h, i: (0, 0)),
        ],
        out_specs=pl.BlockSpec((1, tq, DA_VDIM), lambda bi, h, i: (bi, i, h)),
        scratch_shapes=[
            pltpu.VMEM((2, DA_VDIM, tq), F32),
            pltpu.VMEM((2, 2, 2 * DA_HEAD_DIM, tq), BF16),
            pltpu.VMEM((2, s, 2 * DA_HEAD_DIM), BF16),
            pltpu.VMEM((s // tk, DA_VDIM, tk), BF16),
            pltpu.VMEM((tk, tq), F32),
            pltpu.VMEM((2, 2, tk, tq), F32),
        ],
        compiler_params=_params(("parallel", "parallel", "arbitrary")),
        name="diff_attn",
    )(jnp.asarray(csum), lam4, jnp.asarray(aq), jnp.asarray(ak), proj, proj, proj, g_subln)


def _na_variant_rows(variant, ri, kr):
    if variant == 0:
        valid, dr = 0 <= kr < NA_ROWS, kr - ri + NA_ROWS - 1
    elif variant == 1:
        valid, dr = ri <= kr < ri + NA_ROWS, kr - ri + NA_ROWS - 1 - NA_GROUP
    else:
        valid, dr = NA_GROUP <= kr < NA_KROWS, kr - ri - 1
    return dr if valid else None


def _na_bias_kernel(rpb_ref, tab_ref, tt_ref):
    h = pl.program_id(0)
    shape = (GRID_W, 2 * GRID_W)
    lane = lax.broadcasted_iota(jnp.int32, shape, 1)
    kcol = lax.broadcasted_iota(jnp.int32, shape, 0)
    col = lane & (GRID_W - 1)
    rel = kcol - col + (NA_COLS - 1)
    cstart = jnp.clip(col - NA_COLS // 2, 0, GRID_W - NA_COLS)
    band = (kcol >= cstart) & (kcol < cstart + NA_COLS)
    neg = jnp.full(shape, NEG, F32)
    for dr in range(NA_RPB_R):
        t = neg
        for dc in range(NA_RPB_C):
            val = rpb_ref[h * (NA_RPB_R * NA_RPB_C) + dr * NA_RPB_C + dc]
            t = jnp.where(rel == dc, val, t)
        tt_ref[dr] = jnp.where(band, t, neg)
    first_half = lane < GRID_W
    for variant in range(3):
        for kr in range(NA_KROWS):
            for pair in range(NA_GROUP // 2):
                da = _na_variant_rows(variant, 2 * pair, kr)
                db = _na_variant_rows(variant, 2 * pair + 1, kr)
                ta = neg if da is None else tt_ref[da]
                tb = neg if db is None else tt_ref[db]
                tab_ref[0, variant, kr * GRID_W:(kr + 1) * GRID_W,
                        pair * 2 * GRID_W:(pair + 1) * 2 * GRID_W] = jnp.where(first_half, ta, tb)


def _na_bias(rpb_flat):
    return pl.pallas_call(
        _na_bias_kernel,
        out_shape=jax.ShapeDtypeStruct((NA_HEADS, 3, NA_K, NA_Q), F32),
        grid=(NA_HEADS,),
        in_specs=[pl.BlockSpec(memory_space=pltpu.SMEM)],
        out_specs=pl.BlockSpec((1, 3, NA_K, NA_Q), lambda h: (h, 0, 0, 0)),
        scratch_shapes=[pltpu.VMEM((NA_RPB_R, GRID_W, 2 * GRID_W), F32)],
        compiler_params=_params(("arbitrary",)),
        name="na_bias",
    )(rpb_flat)


def _na_kernel(q_ref, k_ref, v_ref, tab_ref, o_ref, vt_ref, s_ref, *, n_rows):
    n_groups = n_rows // NA_GROUP
    n_blocks = NA_KROWS // NA_GROUP
    scale = NA_HEAD_DIM ** -0.5

    def transpose_values(g, carry):
        r0 = pl.multiple_of(g * NA_Q, NA_Q)
        vt_ref[g] = v_ref[0, pl.ds(r0, NA_Q), :].astype(F32).T.astype(BF16)
        return carry

    lax.fori_loop(0, n_groups, transpose_values, 0)

    def window(g):
        return jnp.clip(g - 1, 0, n_groups - n_blocks)

    def scores(g, variant, slot):
        start = pl.multiple_of(window(g) * NA_Q, NA_Q)
        q0 = pl.multiple_of(g * NA_Q, NA_Q)
        kw = k_ref[0, pl.ds(start, NA_K), :]
        s = lax.dot_general(kw, q_ref[0, pl.ds(q0, NA_Q), :], (((1,), (1,)), ((), ())),
                            preferred_element_type=F32)
        s = s * scale + tab_ref[0, variant]
        s_ref[slot] = s
        return jnp.max(s, axis=0, keepdims=True)

    def finish(g, slot, m):
        kb = window(g)
        q0 = pl.multiple_of(g * NA_Q, NA_Q)
        p = jnp.exp(s_ref[slot] - m)
        l = jnp.sum(p, axis=0, keepdims=True)
        pb = p.astype(BF16)
        ot = jnp.dot(vt_ref[kb], pb[0:NA_Q], preferred_element_type=F32)
        for t in range(1, n_blocks):
            ot += jnp.dot(vt_ref[kb + t], pb[t * NA_Q:(t + 1) * NA_Q], preferred_element_type=F32)
        o_ref[0, pl.ds(q0, NA_Q), :] = (ot / l).T.astype(BF16)

    m_first = scores(0, 0, 0)

    def pair(t, carry):
        g_prev, m_prev = carry
        g_a = 1 + 2 * t
        m_a = scores(g_a, 1, 1)
        finish(g_prev, 0, m_prev)
        g_b = g_a + 1
        m_b = scores(g_b, 1, 0)
        finish(g_a, 1, m_a)
        return g_b, m_b

    g_prev, m_prev = lax.fori_loop(0, (n_groups - 2) // 2, pair, (0, m_first))
    m_last = scores(n_groups - 1, 2, 1)
    finish(g_prev, 0, m_prev)
    finish(n_groups - 1, 1, m_last)


def _neighborhood_attention(proj, table):
    b, s, _ = proj.shape
    n_rows = s // GRID_W
    assert n_rows % NA_GROUP == 0 and n_rows >= NA_KROWS and NA_ROWS // 2 == NA_GROUP
    assert NA_KROWS % NA_GROUP == 0 and (n_rows // NA_GROUP) % 2 == 0
    qb, kb, vb = COL_QN // NA_HEAD_DIM, COL_KN // NA_HEAD_DIM, COL_VN // NA_HEAD_DIM
    return pl.pallas_call(
        functools.partial(_na_kernel, n_rows=n_rows),
        out_shape=jax.ShapeDtypeStruct((b, s, NA_WIDTH), BF16),
        grid=(NA_HEADS, b),
        in_specs=[
            pl.BlockSpec((1, s, NA_HEAD_DIM), lambda h, bi: (bi, 0, qb + h)),
            pl.BlockSpec((1, s, NA_HEAD_DIM), lambda h, bi: (bi, 0, kb + h)),
            pl.BlockSpec((1, s, NA_HEAD_DIM), lambda h, bi: (bi, 0, vb + h)),
            pl.BlockSpec((1, 3, NA_K, NA_Q), lambda h, bi: (h, 0, 0, 0)),
        ],
        out_specs=pl.BlockSpec((1, s, NA_HEAD_DIM), lambda h, bi: (bi, 0, h)),
        scratch_shapes=[pltpu.VMEM((n_rows // NA_GROUP, NA_HEAD_DIM, NA_Q), BF16),
                        pltpu.VMEM((2, NA_K, NA_Q), F32)],
        compiler_params=_params(("parallel", "parallel")),
        name="nbr_attn",
    )(proj, proj, proj, table)


def _sigmoid(x):
    return 1.0 / (1.0 + jnp.exp(-x))


def _merge_kernel(x_ref, oa_ref, on_ref, ga_ref, gb_ref, wpa_ref, wpb_ref, wout_ref,
                  gt1_ref, gmlp_ref, sc2_ref, sh2_ref, x1_ref, h2_ref):
    pa = jnp.dot(oa_ref[0], wpa_ref[...], preferred_element_type=F32)
    pb = jnp.dot(on_ref[0], wpb_ref[...], preferred_element_type=F32)
    merged = _sigmoid(ga_ref[0].astype(F32)) * pa + _sigmoid(gb_ref[0].astype(F32)) * pb
    out = jnp.dot(merged.astype(BF16), wout_ref[...], preferred_element_type=F32)
    x1 = x_ref[0] + gt1_ref[0] * out
    x1_ref[0] = x1
    ms = jnp.mean(x1 * x1, axis=-1, keepdims=True)
    y = x1 * lax.rsqrt(ms + EPS) * gmlp_ref[...]
    h2_ref[0] = (y * (1.0 + sc2_ref[0]) + sh2_ref[0]).astype(BF16)


def _merge(x, oa, on, proj, w_pa, w_pb, w_out, ada3, g_mlp, tm=512):
    b, s, _ = x.shape
    ga_b, gb_b = COL_GA // D_MODEL, COL_GB // D_MODEL
    once = pl.Buffered(1)

    def ada_spec(idx):
        return pl.BlockSpec((1, 1, D_MODEL), lambda bi, i: (bi * 6 + idx, 0, 0))

    return pl.pallas_call(
        _merge_kernel,
        out_shape=(jax.ShapeDtypeStruct((b, s, D_MODEL), F32),
                   jax.ShapeDtypeStruct((b, s, D_MODEL), BF16)),
        grid=(b, s // tm),
        in_specs=[
            pl.BlockSpec((1, tm, D_MODEL), lambda bi, i: (bi, i, 0)),
            pl.BlockSpec((1, tm, DA_WIDTH), lambda bi, i: (bi, i, 0)),
            pl.BlockSpec((1, tm, NA_WIDTH), lambda bi, i: (bi, i, 0)),
            pl.BlockSpec((1, tm, D_MODEL), lambda bi, i: (bi, i, ga_b)),
            pl.BlockSpec((1, tm, D_MODEL), lambda bi, i: (bi, i, gb_b)),
            pl.BlockSpec((DA_WIDTH, D_MODEL), lambda bi, i: (0, 0), pipeline_mode=once),
            pl.BlockSpec((NA_WIDTH, D_MODEL), lambda bi, i: (0, 0), pipeline_mode=once),
            pl.BlockSpec((D_MODEL, D_MODEL), lambda bi, i: (0, 0), pipeline_mode=once),
            ada_spec(2),
            pl.BlockSpec((1, D_MODEL), lambda bi, i: (0, 0)),
            ada_spec(4),
            ada_spec(3),
        ],
        out_specs=(pl.BlockSpec((1, tm, D_MODEL), lambda bi, i: (bi, i, 0)),
                   pl.BlockSpec((1, tm, D_MODEL), lambda bi, i: (bi, i, 0))),
        compiler_params=_params(("parallel", "parallel")),
        name="merge_out_proj",
    )(x, oa, on, proj, proj, w_pa, w_pb, w_out, ada3, g_mlp, ada3, ada3)


def _mlp_kernel(h2_ref, w1_ref, w2_ref, x1_ref, gt2_ref, gf_ref, y_ref, acc_ref):
    f = pl.program_id(2)
    u = jnp.dot(h2_ref[0], w1_ref[...], preferred_element_type=F32)
    u = jnp.square(jnp.maximum(u, 0.0)).astype(BF16)
    contrib = jnp.dot(u, w2_ref[...], preferred_element_type=F32)

    @pl.when(f == 0)
    def _():
        acc_ref[...] = contrib

    @pl.when(f != 0)
    def _():
        acc_ref[...] += contrib

    @pl.when(f == pl.num_programs(2) - 1)
    def _():
        x2 = x1_ref[0] + gt2_ref[0] * acc_ref[...]
        ms = jnp.mean(x2 * x2, axis=-1, keepdims=True)
        y_ref[0] = x2 * lax.rsqrt(ms + EPS) * gf_ref[...]


def _mlp(h2, x1, w1, w2, ada3, g_final, tm=512, tf=1024):
    b, s, _ = x1.shape
    return pl.pallas_call(
        _mlp_kernel,
        out_shape=jax.ShapeDtypeStruct((b, s, D_MODEL), F32),
        grid=(b, s // tm, D_FF // tf),
        in_specs=[
            pl.BlockSpec((1, tm, D_MODEL), lambda bi, i, f: (bi, i, 0)),
            pl.BlockSpec((D_MODEL, tf), lambda bi, i, f: (0, f)),
            pl.BlockSpec((tf, D_MODEL), lambda bi, i, f: (f, 0)),
            pl.BlockSpec((1, tm, D_MODEL), lambda bi, i, f: (bi, i, 0)),
            pl.BlockSpec((1, 1, D_MODEL), lambda bi, i, f: (bi * 6 + 5, 0, 0)),
            pl.BlockSpec((1, D_MODEL), lambda bi, i, f: (0, 0)),
        ],
        out_specs=pl.BlockSpec((1, tm, D_MODEL), lambda bi, i, f: (bi, i, 0)),
        scratch_shapes=[pltpu.VMEM((tm, D_MODEL), F32)],
        compiler_params=_params(("parallel", "parallel", "arbitrary")),
        name="mlp_final_norm",
    )(h2, w1, w2, x1, ada3, g_final)


def _trunk(x, ada, w, table, lam4):
    b = x.shape[0]
    ada3 = ada.reshape(b * 6, 1, D_MODEL)
    proj = _inproj(x, ada3, w["g_mix"], w["w_in"])
    oa = _diff_attention(proj, lam4, w["g_subln"])
    on = _neighborhood_attention(proj, table)
    x1, h2 = _merge(x, oa, on, proj, w["w_pa"], w["w_pb"], w["w_out"], ada3, w["g_mlp"])
    return _mlp(h2, x1, w["w1"], w["w2"], ada3, w["g_final"])


def kernel(x_prompt, x_sample, c_prompt, c_sample, w_ada, b_ada, g_mix, w_in, lam_q1, lam_k1, lam_q2, lam_k2, g_subln, rpb, w_pa, w_pb, w_out, g_mlp, w1, w2, g_final):
    w = {
        "g_mix": g_mix[0].reshape(1, D_MODEL),
        "w_in": w_in[0].astype(BF16),
        "g_subln": g_subln[0].reshape(1, DA_VDIM),
        "w_pa": w_pa[0].astype(BF16),
        "w_pb": w_pb[0].astype(BF16),
        "w_out": w_out[0].astype(BF16),
        "g_mlp": g_mlp[0].reshape(1, D_MODEL),
        "w1": w1[0].astype(BF16),
        "w2": w2[0].astype(BF16),
        "g_final": g_final.reshape(1, D_MODEL),
    }
    nb_p = c_prompt.shape[0]
    ada = _ada(jnp.concatenate([c_prompt, c_sample], axis=0), w_ada[0], b_ada[0])
    table = _na_bias(rpb[0].reshape(-1))
    lam4 = jnp.stack([lam_q1[0], lam_k1[0], lam_q2[0], lam_k2[0]], axis=0)
    y_prompt = _trunk(x_prompt, ada[:nb_p], w, table, lam4)
    y_sample = _trunk(x_sample, ada[nb_p:], w, table, lam4)
    return (y_prompt, y_sample)
```

```python
import functools
import math

import jax
import jax.numpy as jnp
import ml_dtypes
import numpy as np
from jax import lax
from jax.experimental import pallas as pl
from jax.experimental.pallas import tpu as pltpu

F32 = jnp.float32
BF16 = jnp.bfloat16

D_MODEL = 2048
DA_HEADS = 4
DA_HEAD_DIM = 128
DA_VDIM = 256
DA_WIDTH = DA_HEADS * DA_VDIM
NA_HEADS = 8
NA_HEAD_DIM = 128
NA_WIDTH = NA_HEADS * NA_HEAD_DIM
GRID_W = 64
NA_ROWS = 8
NA_COLS = 16
D_FF = 4 * D_MODEL
ALIBI_MAX_BIAS = 8.0
EPS = 1e-6
IN_W = 3 * DA_WIDTH + 3 * NA_WIDTH + 2 * D_MODEL
LAMBDA_INIT = 0.8 - 0.6 * math.exp(-0.3 * 0)

COL_QA, COL_KA, COL_VA = 0, DA_WIDTH, 2 * DA_WIDTH
COL_QN = 3 * DA_WIDTH
COL_KN = COL_QN + NA_WIDTH
COL_VN = COL_KN + NA_WIDTH
COL_GA = COL_VN + NA_WIDTH
COL_GB = COL_GA + D_MODEL

NA_GROUP = 4
NA_KROWS = NA_GROUP + NA_ROWS
NA_Q = NA_GROUP * GRID_W
NA_K = NA_KROWS * GRID_W
NA_RPB_R = 2 * NA_ROWS - 1
NA_RPB_C = 2 * NA_COLS - 1
NEG = -1e30

VMEM_LIMIT = 56 * 1024 * 1024


def _params(sem):
    return pltpu.CompilerParams(dimension_semantics=sem, vmem_limit_bytes=VMEM_LIMIT)


def _ada_kernel(c_ref, w_ref, b_ref, o_ref):
    c = c_ref[...].astype(BF16)
    w = w_ref[...].astype(BF16)
    o_ref[...] = jnp.dot(c, w, preferred_element_type=F32) + b_ref[...]


def _ada(c_all, w_ada, b_ada, tn=1024):
    nb = c_all.shape[0]
    n = w_ada.shape[1]
    return pl.pallas_call(
        _ada_kernel,
        out_shape=jax.ShapeDtypeStruct((nb, n), F32),
        grid=(n // tn,),
        in_specs=[
            pl.BlockSpec((nb, D_MODEL), lambda j: (0, 0)),
            pl.BlockSpec((D_MODEL, tn), lambda j: (0, j)),
            pl.BlockSpec((1, tn), lambda j: (0, j)),
        ],
        out_specs=pl.BlockSpec((nb, tn), lambda j: (0, j)),
        compiler_params=_params(("parallel",)),
        name="ada_proj",
    )(c_all, w_ada, b_ada.reshape(1, n))


def _modulated_norm(x, g, sc, sh):
    ms = jnp.mean(x * x, axis=-1, keepdims=True)
    return ((x * lax.rsqrt(ms + EPS) * g) * (1.0 + sc) + sh).astype(BF16)


def _inproj_kernel(x_ref, g_ref, sc_ref, sh_ref, w_ref, o_ref, h_ref, *, rows):
    @pl.when(pl.program_id(2) == 0)
    def _():
        for c in range(x_ref.shape[1] // rows):
            sl = pl.ds(c * rows, rows)
            h = _modulated_norm(x_ref[0, sl, :], g_ref[...], sc_ref[0], sh_ref[0])
            h_ref[sl, :] = h
            o_ref[0, sl, :] = jnp.dot(h, w_ref[...], preferred_element_type=F32).astype(BF16)

    @pl.when(pl.program_id(2) != 0)
    def _():
        o_ref[0] = jnp.dot(h_ref[...], w_ref[...], preferred_element_type=F32).astype(BF16)


def _inproj(x, ada3, g_mix, w_in, tm=1024, tn=2048, rows=256):
    b, s, _ = x.shape
    return pl.pallas_call(
        functools.partial(_inproj_kernel, rows=rows),
        out_shape=jax.ShapeDtypeStruct((b, s, IN_W), BF16),
        grid=(b, s // tm, IN_W // tn),
        in_specs=[
            pl.BlockSpec((1, tm, D_MODEL), lambda bi, i, n: (bi, i, 0)),
            pl.BlockSpec((1, D_MODEL), lambda bi, i, n: (0, 0)),
            pl.BlockSpec((1, 1, D_MODEL), lambda bi, i, n: (bi * 6 + 1, 0, 0)),
            pl.BlockSpec((1, 1, D_MODEL), lambda bi, i, n: (bi * 6 + 0, 0, 0)),
            pl.BlockSpec((D_MODEL, tn), lambda bi, i, n: (0, n)),
        ],
        out_specs=pl.BlockSpec((1, tm, tn), lambda bi, i, n: (bi, i, n)),
        scratch_shapes=[pltpu.VMEM((tm, D_MODEL), BF16)],
        compiler_params=_params(("parallel", "parallel", "arbitrary")),
        name="in_proj",
    )(x, g_mix, ada3, ada3, w_in)


DA_NCH = 4


def _alibi_lane_constants():
    csum = np.zeros((DA_HEADS,), np.float32)
    aq = np.zeros((DA_HEADS, DA_HEAD_DIM, 1), np.float32)
    ak = np.zeros((DA_HEADS, 1, DA_HEAD_DIM), np.float32)
    for h in range(DA_HEADS):
        slope = 2.0 ** (-ALIBI_MAX_BIAS * (h + 1) / DA_HEADS)
        rem = slope * math.sqrt(DA_HEAD_DIM)
        pieces = []
        for _ in range(DA_NCH):
            p = float(np.float32(rem).astype(ml_dtypes.bfloat16).astype(np.float32))
            pieces.append(p)
            rem -= p
        csum[h] = np.float32(sum(pieces))
        for d, weight in enumerate((256.0, 16.0, 1.0)):
            for n, p in enumerate(pieces):
                ak[h, 0, d * DA_NCH + n] = -weight * p
                aq[h, (3 + d) * DA_NCH + n, 0] = p
    return csum, aq, ak


def _da_kernel(csum_ref, lam_ref, aq_ref, ak_ref, q_ref, k_ref, v_ref, g_ref, o_ref,
               acc_ref, qt_ref, ka_ref, vt_ref, corr_ref, s_ref, *, tk):
    h = pl.program_id(1)
    i = pl.program_id(2)
    tq = q_ref.shape[1]
    s_len = k_ref.shape[1]
    scale = DA_HEAD_DIM ** -0.5
    n3 = 3 * DA_NCH

    lq = lam_ref[...]
    e1 = jnp.exp(jnp.sum(lq[0:1] * lq[1:2], axis=-1, keepdims=True))
    e2 = jnp.exp(jnp.sum(lq[2:3] * lq[3:4], axis=-1, keepdims=True))
    lam = e1 - e2 + LAMBDA_INIT

    @pl.when(i == 0)
    def _():
        ck = ak_ref[0]
        lane = lax.broadcasted_iota(jnp.int32, (tk, DA_HEAD_DIM), 1)
        row = lax.broadcasted_iota(jnp.int32, (tk, DA_HEAD_DIM), 0)

        def build(r, carry):
            r0 = pl.multiple_of(r * tk, tk)
            pos = row + r * tk
            l2 = lane - n3
            dig = jnp.where(l2 < DA_NCH, (pos >> 8) << 8,
                            jnp.where(l2 < 2 * DA_NCH, ((pos >> 4) & 15) << 4, pos & 15)).astype(F32)
            kpos = jnp.where((lane >= n3) & (lane < 2 * n3), dig, ck).astype(BF16)
            for c in range(2):
                ka_ref[c, pl.ds(r0, tk), 0:DA_HEAD_DIM] = k_ref[0, pl.ds(r0, tk), c * DA_HEAD_DIM:(c + 1) * DA_HEAD_DIM]
                ka_ref[c, pl.ds(r0, tk), DA_HEAD_DIM:] = kpos
            vt_ref[r] = v_ref[0, pl.ds(r0, tk), :].astype(F32).T.astype(BF16)
            return carry

        lax.fori_loop(0, s_len // tk, build, 0)
        gap = (lax.broadcasted_iota(jnp.int32, (tk, tq), 0)
               - lax.broadcasted_iota(jnp.int32, (tk, tq), 1))
        corr_ref[...] = jnp.maximum(gap, 0).astype(F32) * (-2.0 * csum_ref[h])

    prow = lax.broadcasted_iota(jnp.int32, (DA_HEAD_DIM, tq), 0)
    pos = lax.broadcasted_iota(jnp.int32, (DA_HEAD_DIM, tq), 1) + i * tq
    dig = jnp.where(prow < DA_NCH, pos >> 8, jnp.where(prow < 2 * DA_NCH, (pos >> 4) & 15, pos & 15)).astype(F32)
    qpos = jnp.where(prow < n3, dig, aq_ref[0])
    for c in range(2):
        qct = q_ref[0, :, c * DA_HEAD_DIM:(c + 1) * DA_HEAD_DIM].astype(F32).T.astype(BF16)
        for sign in range(2):
            qt_ref[sign, c, 0:DA_HEAD_DIM, :] = qct
            qt_ref[sign, c, DA_HEAD_DIM:, :] = (qpos if sign == 0 else -qpos).astype(BF16)
    acc_ref[...] = jnp.zeros_like(acc_ref)

    c2 = scale * math.log2(math.e)

    def scores(j, slot, diag=False):
        j0 = pl.multiple_of(j * tk, tk)
        sign = (j > i).astype(jnp.int32)
        maxes = []
        for c in range(2):
            s = jnp.dot(ka_ref[c, pl.ds(j0, tk), :], qt_ref[sign, c], preferred_element_type=F32)
            if diag:
                s = s + corr_ref[...]
            s_ref[slot, c] = s
            maxes.append(jnp.max(s, axis=0, keepdims=True))
        return tuple(maxes)

    def consume(j, slot, maxes, stats):
        vt = vt_ref[j]
        out = []
        for c in range(2):
            m, l = stats[2 * c], stats[2 * c + 1]
            m_new = jnp.maximum(m, maxes[c])
            alpha = jnp.exp2((m - m_new) * c2)
            p = jnp.exp2((s_ref[slot, c] - m_new) * c2)
            l_new = alpha * l + jnp.sum(p, axis=0, keepdims=True)
            acc_ref[c] = alpha * acc_ref[c] + jnp.dot(vt, p.astype(BF16), preferred_element_type=F32)
            out += [m_new, l_new]
        return tuple(out)

    nk = s_len // tk

    def tile_of(t):
        u = t - 1
        return u + (u >= i).astype(jnp.int32)

    m_init = jnp.full((1, tq), NEG, F32)
    l_init = jnp.zeros((1, tq), F32)
    mx = scores(i, 0, diag=True)

    def pair(p, carry):
        j_a, mx_a, stats = carry
        j_b = tile_of(2 * p + 1)
        mx_b = scores(j_b, 1)
        stats = consume(j_a, 0, mx_a, stats)
        j_c = tile_of(2 * p + 2)
        mx_c = scores(j_c, 0)
        stats = consume(j_b, 1, mx_b, stats)
        return j_c, mx_c, stats

    j_a, mx_a, stats = lax.fori_loop(0, nk // 2 - 1, pair, (i, mx, (m_init, l_init, m_init, l_init)))
    j_b = tile_of(nk - 1)
    mx_b = scores(j_b, 1)
    stats = consume(j_a, 0, mx_a, stats)
    m0, l0, m1, l1 = consume(j_b, 1, mx_b, stats)

    ot = acc_ref[0] / l0 - lam * (acc_ref[1] / l1)
    ms = jnp.mean(ot * ot, axis=0, keepdims=True)
    y = (ot * lax.rsqrt(ms + EPS)).T * g_ref[...] * (1.0 - LAMBDA_INIT)
    o_ref[0] = y.astype(BF16)


def _diff_attention(proj, lam4, g_subln, tq=512):
    b, s, _ = proj.shape
    tk = tq
    assert s % (2 * tk) == 0 and s <= 16 ** 3
    qb, kb, vb = COL_QA // DA_VDIM, COL_KA // DA_VDIM, COL_VA // DA_VDIM
    csum, aq, ak = _alibi_lane_constants()
    return pl.pallas_call(
        functools.partial(_da_kernel, tk=tk),
        out_shape=jax.ShapeDtypeStruct((b, s, DA_WIDTH), BF16),
        grid=(b, DA_HEADS, s // tq),
        in_specs=[
            pl.BlockSpec(memory_space=pltpu.SMEM),
            pl.BlockSpec((4, DA_HEAD_DIM), lambda bi, h, i: (0, 0)),
            pl.BlockSpec((1, DA_HEAD_DIM, 1), lambda bi, h, i: (h, 0, 0)),
            pl.BlockSpec((1, 1, DA_HEAD_DIM), lambda bi, h, i: (h, 0, 0)),
            pl.BlockSpec((1, tq, DA_VDIM), lambda bi, h, i: (bi, i, qb + h)),
            pl.BlockSpec((1, s, DA_VDIM), lambda bi, h, i: (bi, 0, kb + h)),
            pl.BlockSpec((1, s, DA_VDIM), lambda bi, h, i: (bi, 0, vb + h)),
            pl.BlockSpec((1, DA_VDIM), lambda bi, h, i: (0, 0)),
        ],
        out_specs=pl.BlockSpec((1, tq, DA_VDIM), lambda bi, h, i: (bi, i, h)),
        scratch_shapes=[
            pltpu.VMEM((2, DA_VDIM, tq), F32),
            pltpu.VMEM((2, 2, 2 * DA_HEAD_DIM, tq), BF16),
            pltpu.VMEM((2, s, 2 * DA_HEAD_DIM), BF16),
            pltpu.VMEM((s // tk, DA_VDIM, tk), BF16),
            pltpu.VMEM((tk, tq), F32),
            pltpu.VMEM((2, 2, tk, tq), F32),
        ],
        compiler_params=_params(("parallel", "parallel", "arbitrary")),
        name="diff_attn",
    )(jnp.asarray(csum), lam4, jnp.asarray(aq), jnp.asarray(ak), proj, proj, proj, g_subln)


def _na_variant_rows(variant, ri, kr):
    if variant == 0:
        valid, dr = 0 <= kr < NA_ROWS, kr - ri + NA_ROWS - 1
    elif variant == 1:
        valid, dr = ri <= kr < ri + NA_ROWS, kr - ri + NA_ROWS - 1 - NA_GROUP
    else:
        valid, dr = NA_GROUP <= kr < NA_KROWS, kr - ri - 1
    return dr if valid else None


def _na_bias_kernel(rpb_ref, tab_ref, tt_ref):
    h = pl.program_id(0)
    shape = (GRID_W, 2 * GRID_W)
    lane = lax.broadcasted_iota(jnp.int32, shape, 1)
    kcol = lax.broadcasted_iota(jnp.int32, shape, 0)
    col = lane & (GRID_W - 1)
    rel = kcol - col + (NA_COLS - 1)
    cstart = jnp.clip(col - NA_COLS // 2, 0, GRID_W - NA_COLS)
    band = (kcol >= cstart) & (kcol < cstart + NA_COLS)
    neg = jnp.full(shape, NEG, F32)
    for dr in range(NA_RPB_R):
        t = neg
        for dc in range(NA_RPB_C):
            val = rpb_ref[h * (NA_RPB_R * NA_RPB_C) + dr * NA_RPB_C + dc]
            t = jnp.where(rel == dc, val, t)
        tt_ref[dr] = jnp.where(band, t * math.log2(math.e), neg)
    first_half = lane < GRID_W
    for variant in range(3):
        for kr in range(NA_KROWS):
            for pair in range(NA_GROUP // 2):
                da = _na_variant_rows(variant, 2 * pair, kr)
                db = _na_variant_rows(variant, 2 * pair + 1, kr)
                ta = neg if da is None else tt_ref[da]
                tb = neg if db is None else tt_ref[db]
                tab_ref[0, variant, kr * GRID_W:(kr + 1) * GRID_W,
                        pair * 2 * GRID_W:(pair + 1) * 2 * GRID_W] = jnp.where(first_half, ta, tb)


def _na_bias(rpb_flat):
    return pl.pallas_call(
        _na_bias_kernel,
        out_shape=jax.ShapeDtypeStruct((NA_HEADS, 3, NA_K, NA_Q), F32),
        grid=(NA_HEADS,),
        in_specs=[pl.BlockSpec(memory_space=pltpu.SMEM)],
        out_specs=pl.BlockSpec((1, 3, NA_K, NA_Q), lambda h: (h, 0, 0, 0)),
        scratch_shapes=[pltpu.VMEM((NA_RPB_R, GRID_W, 2 * GRID_W), F32)],
        compiler_params=_params(("arbitrary",)),
        name="na_bias",
    )(rpb_flat)


def _na_kernel(q_ref, k_ref, v_ref, tab_ref, o_ref, vt_ref, s_ref, *, n_rows):
    n_groups = n_rows // NA_GROUP
    n_blocks = NA_KROWS // NA_GROUP
    c2 = (NA_HEAD_DIM ** -0.5) * math.log2(math.e)

    def transpose_values(g):
        g = jnp.minimum(g, n_groups - 1)
        r0 = pl.multiple_of(g * NA_Q, NA_Q)
        vt_ref[g] = v_ref[0, pl.ds(r0, NA_Q), :].astype(F32).T.astype(BF16)

    def window(g):
        return jnp.clip(g - 1, 0, n_groups - n_blocks)

    def scores(g, variant, slot):
        start = pl.multiple_of(window(g) * NA_Q, NA_Q)
        q0 = pl.multiple_of(g * NA_Q, NA_Q)
        kw = k_ref[0, pl.ds(start, NA_K), :]
        s = lax.dot_general(kw, q_ref[0, pl.ds(q0, NA_Q), :], (((1,), (1,)), ((), ())),
                            preferred_element_type=F32)
        s = s * c2 + tab_ref[0, variant]
        s_ref[slot] = s
        return jnp.max(s, axis=0, keepdims=True)

    def finish(g, slot, m):
        kb = window(g)
        q0 = pl.multiple_of(g * NA_Q, NA_Q)
        p = jnp.exp2(s_ref[slot] - m)
        l = jnp.sum(p, axis=0, keepdims=True)
        pb = p.astype(BF16)
        ot = jnp.dot(vt_ref[kb], pb[0:NA_Q], preferred_element_type=F32)
        for t in range(1, n_blocks):
            ot += jnp.dot(vt_ref[kb + t], pb[t * NA_Q:(t + 1) * NA_Q], preferred_element_type=F32)
        o_ref[0, pl.ds(q0, NA_Q), :] = (ot / l).T.astype(BF16)

    for g in range(n_blocks):
        transpose_values(g)
    m_first = scores(0, 0, 0)

    def pair(t, carry):
        g_prev, m_prev = carry
        g_a = 1 + 2 * t
        transpose_values(g_a + 2)
        transpose_values(g_a + 3)
        m_a = scores(g_a, 1, 1)
        finish(g_prev, 0, m_prev)
        g_b = g_a + 1
        m_b = scores(g_b, 1, 0)
        finish(g_a, 1, m_a)
        return g_b, m_b

    g_prev, m_prev = lax.fori_loop(0, (n_groups - 2) // 2, pair, (0, m_first))
    m_last = scores(n_groups - 1, 2, 1)
    finish(g_prev, 0, m_prev)
    finish(n_groups - 1, 1, m_last)


def _neighborhood_attention(proj, table):
    b, s, _ = proj.shape
    n_rows = s // GRID_W
    assert n_rows % NA_GROUP == 0 and n_rows >= NA_KROWS and NA_ROWS // 2 == NA_GROUP
    assert NA_KROWS % NA_GROUP == 0 and (n_rows // NA_GROUP) % 2 == 0
    qb, kb, vb = COL_QN // NA_HEAD_DIM, COL_KN // NA_HEAD_DIM, COL_VN // NA_HEAD_DIM
    return pl.pallas_call(
        functools.partial(_na_kernel, n_rows=n_rows),
        out_shape=jax.ShapeDtypeStruct((b, s, NA_WIDTH), BF16),
        grid=(NA_HEADS, b),
        in_specs=[
            pl.BlockSpec((1, s, NA_HEAD_DIM), lambda h, bi: (bi, 0, qb + h)),
            pl.BlockSpec((1, s, NA_HEAD_DIM), lambda h, bi: (bi, 0, kb + h)),
            pl.BlockSpec((1, s, NA_HEAD_DIM), lambda h, bi: (bi, 0, vb + h)),
            pl.BlockSpec((1, 3, NA_K, NA_Q), lambda h, bi: (h, 0, 0, 0)),
        ],
        out_specs=pl.BlockSpec((1, s, NA_HEAD_DIM), lambda h, bi: (bi, 0, h)),
        scratch_shapes=[pltpu.VMEM((n_rows // NA_GROUP, NA_HEAD_DIM, NA_Q), BF16),
                        pltpu.VMEM((2, NA_K, NA_Q), F32)],
        compiler_params=_params(("parallel", "parallel")),
        name="nbr_attn",
    )(proj, proj, proj, table)


def _sigmoid(x):
    return 1.0 / (1.0 + jnp.exp(-x))


def _merge_kernel(x_ref, oa_ref, on_ref, ga_ref, gb_ref, wpa_ref, wpb_ref, wout_ref,
                  gt1_ref, gmlp_ref, sc2_ref, sh2_ref, x1_ref, h2_ref):
    pa = jnp.dot(oa_ref[0], wpa_ref[...], preferred_element_type=F32)
    pb = jnp.dot(on_ref[0], wpb_ref[...], preferred_element_type=F32)
    merged = _sigmoid(ga_ref[0].astype(F32)) * pa + _sigmoid(gb_ref[0].astype(F32)) * pb
    out = jnp.dot(merged.astype(BF16), wout_ref[...], preferred_element_type=F32)
    x1 = x_ref[0] + gt1_ref[0] * out
    x1_ref[0] = x1
    ms = jnp.mean(x1 * x1, axis=-1, keepdims=True)
    y = x1 * lax.rsqrt(ms + EPS) * gmlp_ref[...]
    h2_ref[0] = (y * (1.0 + sc2_ref[0]) + sh2_ref[0]).astype(BF16)


def _merge(x, oa, on, proj, w_pa, w_pb, w_out, ada3, g_mlp, tm=512):
    b, s, _ = x.shape
    ga_b, gb_b = COL_GA // D_MODEL, COL_GB // D_MODEL
    once = pl.Buffered(1)

    def ada_spec(idx):
        return pl.BlockSpec((1, 1, D_MODEL), lambda bi, i: (bi * 6 + idx, 0, 0))

    return pl.pallas_call(
        _merge_kernel,
        out_shape=(jax.ShapeDtypeStruct((b, s, D_MODEL), F32),
                   jax.ShapeDtypeStruct((b, s, D_MODEL), BF16)),
        grid=(b, s // tm),
        in_specs=[
            pl.BlockSpec((1, tm, D_MODEL), lambda bi, i: (bi, i, 0)),
            pl.BlockSpec((1, tm, DA_WIDTH), lambda bi, i: (bi, i, 0)),
            pl.BlockSpec((1, tm, NA_WIDTH), lambda bi, i: (bi, i, 0)),
            pl.BlockSpec((1, tm, D_MODEL), lambda bi, i: (bi, i, ga_b)),
            pl.BlockSpec((1, tm, D_MODEL), lambda bi, i: (bi, i, gb_b)),
            pl.BlockSpec((DA_WIDTH, D_MODEL), lambda bi, i: (0, 0), pipeline_mode=once),
            pl.BlockSpec((NA_WIDTH, D_MODEL), lambda bi, i: (0, 0), pipeline_mode=once),
            pl.BlockSpec((D_MODEL, D_MODEL), lambda bi, i: (0, 0), pipeline_mode=once),
            ada_spec(2),
            pl.BlockSpec((1, D_MODEL), lambda bi, i: (0, 0)),
            ada_spec(4),
            ada_spec(3),
        ],
        out_specs=(pl.BlockSpec((1, tm, D_MODEL), lambda bi, i: (bi, i, 0)),
                   pl.BlockSpec((1, tm, D_MODEL), lambda bi, i: (bi, i, 0))),
        compiler_params=_params(("parallel", "parallel")),
        name="merge_out_proj",
    )(x, oa, on, proj, proj, w_pa, w_pb, w_out, ada3, g_mlp, ada3, ada3)


def _mlp_kernel(h2_ref, w1_ref, w2_ref, x1_ref, gt2_ref, gf_ref, y_ref):
    f = pl.program_id(2)

    @pl.when(f == 0)
    def _():
        y_ref[...] = jnp.zeros_like(y_ref)

    u = jnp.dot(h2_ref[0], w1_ref[...], preferred_element_type=F32)
    u = jnp.square(jnp.maximum(u, 0.0)).astype(BF16)
    y_ref[0] += jnp.dot(u, w2_ref[...], preferred_element_type=F32)

    @pl.when(f == pl.num_programs(2) - 1)
    def _():
        x2 = x1_ref[0] + gt2_ref[0] * y_ref[0]
        ms = jnp.mean(x2 * x2, axis=-1, keepdims=True)
        y_ref[0] = x2 * lax.rsqrt(ms + EPS) * gf_ref[...]


def _mlp(h2, x1, w1, w2, ada3, g_final, tm=512, tf=1024):
    b, s, _ = x1.shape
    return pl.pallas_call(
        _mlp_kernel,
        out_shape=jax.ShapeDtypeStruct((b, s, D_MODEL), F32),
        grid=(b, s // tm, D_FF // tf),
        in_specs=[
            pl.BlockSpec((1, tm, D_MODEL), lambda bi, i, f: (bi, i, 0)),
            pl.BlockSpec((D_MODEL, tf), lambda bi, i, f: (0, f)),
            pl.BlockSpec((tf, D_MODEL), lambda bi, i, f: (f, 0)),
            pl.BlockSpec((1, tm, D_MODEL), lambda bi, i, f: (bi, i, 0)),
            pl.BlockSpec((1, 1, D_MODEL), lambda bi, i, f: (bi * 6 + 5, 0, 0)),
            pl.BlockSpec((1, D_MODEL), lambda bi, i, f: (0, 0)),
        ],
        out_specs=pl.BlockSpec((1, tm, D_MODEL), lambda bi, i, f: (bi, i, 0)),
        compiler_params=_params(("parallel", "parallel", "arbitrary")),
        name="mlp_final_norm",
    )(h2, w1, w2, x1, ada3, g_final)


def _trunk(x, ada, w, table, lam4):
    b = x.shape[0]
    ada3 = ada.reshape(b * 6, 1, D_MODEL)
    proj = _inproj(x, ada3, w["g_mix"], w["w_in"])
    oa = _diff_attention(proj, lam4, w["g_subln"])
    on = _neighborhood_attention(proj, table)
    x1, h2 = _merge(x, oa, on, proj, w["w_pa"], w["w_pb"], w["w_out"], ada3, w["g_mlp"])
    return _mlp(h2, x1, w["w1"], w["w2"], ada3, w["g_final"])


def kernel(x_prompt, x_sample, c_prompt, c_sample, w_ada, b_ada, g_mix, w_in, lam_q1, lam_k1, lam_q2, lam_k2, g_subln, rpb, w_pa, w_pb, w_out, g_mlp, w1, w2, g_final):
    w = {
        "g_mix": g_mix[0].reshape(1, D_MODEL),
        "w_in": w_in[0].astype(BF16),
        "g_subln": g_subln[0].reshape(1, DA_VDIM),
        "w_pa": w_pa[0].astype(BF16),
        "w_pb": w_pb[0].astype(BF16),
        "w_out": w_out[0].astype(BF16),
        "g_mlp": g_mlp[0].reshape(1, D_MODEL),
        "w1": w1[0].astype(BF16),
        "w2": w2[0].astype(BF16),
        "g_final": g_final.reshape(1, D_MODEL),
    }
    nb_p = c_prompt.shape[0]
    ada = _ada(jnp.concatenate([c_prompt, c_sample], axis=0), w_ada[0], b_ada[0])
    table = _na_bias(rpb[0].reshape(-1))
    lam4 = jnp.stack([lam_q1[0], lam_k1[0], lam_q2[0], lam_k2[0]], axis=0)
    y_prompt = _trunk(x_prompt, ada[:nb_p], w, table, lam4)
    y_sample = _trunk(x_sample, ada[nb_p:], w, table, lam4)
    return (y_prompt, y_sample)
```

```python
import functools
import math

import jax
import jax.numpy as jnp
import ml_dtypes
import numpy as np
from jax import lax
from jax.experimental import pallas as pl
from jax.experimental.pallas import tpu as pltpu

F32 = jnp.float32
BF16 = jnp.bfloat16

D_MODEL = 2048
DA_HEADS = 4
DA_HEAD_DIM = 128
DA_VDIM = 256
DA_WIDTH = DA_HEADS * DA_VDIM
NA_HEADS = 8
NA_HEAD_DIM = 128
NA_WIDTH = NA_HEADS * NA_HEAD_DIM
GRID_W = 64
NA_ROWS = 8
NA_COLS = 16
D_FF = 4 * D_MODEL
ALIBI_MAX_BIAS = 8.0
EPS = 1e-6
IN_W = 3 * DA_WIDTH + 3 * NA_WIDTH + 2 * D_MODEL
LAMBDA_INIT = 0.8 - 0.6 * math.exp(-0.3 * 0)

COL_QA, COL_KA, COL_VA = 0, DA_WIDTH, 2 * DA_WIDTH
COL_QN = 3 * DA_WIDTH
COL_KN = COL_QN + NA_WIDTH
COL_VN = COL_KN + NA_WIDTH
COL_GA = COL_VN + NA_WIDTH
COL_GB = COL_GA + D_MODEL

NA_GROUP = 4
NA_KROWS = NA_GROUP + NA_ROWS
NA_Q = NA_GROUP * GRID_W
NA_K = NA_KROWS * GRID_W
NA_RPB_R = 2 * NA_ROWS - 1
NA_RPB_C = 2 * NA_COLS - 1
NEG = -1e30

VMEM_LIMIT = 56 * 1024 * 1024


def _params(sem):
    return pltpu.CompilerParams(dimension_semantics=sem, vmem_limit_bytes=VMEM_LIMIT)


def _ada_kernel(c_ref, w_ref, b_ref, o_ref):
    c = c_ref[...].astype(BF16)
    w = w_ref[...].astype(BF16)
    o_ref[...] = jnp.dot(c, w, preferred_element_type=F32) + b_ref[...]


def _ada(c_all, w_ada, b_ada, tn=1024):
    nb = c_all.shape[0]
    n = w_ada.shape[1]
    return pl.pallas_call(
        _ada_kernel,
        out_shape=jax.ShapeDtypeStruct((nb, n), F32),
        grid=(n // tn,),
        in_specs=[
            pl.BlockSpec((nb, D_MODEL), lambda j: (0, 0)),
            pl.BlockSpec((D_MODEL, tn), lambda j: (0, j)),
            pl.BlockSpec((1, tn), lambda j: (0, j)),
        ],
        out_specs=pl.BlockSpec((nb, tn), lambda j: (0, j)),
        compiler_params=_params(("parallel",)),
        name="ada_proj",
    )(c_all, w_ada, b_ada.reshape(1, n))


def _modulated_norm(x, g, sc, sh):
    ms = jnp.mean(x * x, axis=-1, keepdims=True)
    return ((x * lax.rsqrt(ms + EPS) * g) * (1.0 + sc) + sh).astype(BF16)


def _inproj_kernel(x_ref, g_ref, sc_ref, sh_ref, w_ref, o_ref, h_ref, *, rows):
    @pl.when(pl.program_id(2) == 0)
    def _():
        for c in range(x_ref.shape[1] // rows):
            sl = pl.ds(c * rows, rows)
            h = _modulated_norm(x_ref[0, sl, :], g_ref[...], sc_ref[0], sh_ref[0])
            h_ref[sl, :] = h
            o_ref[0, sl, :] = jnp.dot(h, w_ref[...], preferred_element_type=F32).astype(BF16)

    @pl.when(pl.program_id(2) != 0)
    def _():
        o_ref[0] = jnp.dot(h_ref[...], w_ref[...], preferred_element_type=F32).astype(BF16)


def _inproj(x, ada3, g_mix, w_in, tm=1024, tn=2048, rows=256):
    b, s, _ = x.shape
    return pl.pallas_call(
        functools.partial(_inproj_kernel, rows=rows),
        out_shape=jax.ShapeDtypeStruct((b, s, IN_W), BF16),
        grid=(b, s // tm, IN_W // tn),
        in_specs=[
            pl.BlockSpec((1, tm, D_MODEL), lambda bi, i, n: (bi, i, 0)),
            pl.BlockSpec((1, D_MODEL), lambda bi, i, n: (0, 0)),
            pl.BlockSpec((1, 1, D_MODEL), lambda bi, i, n: (bi * 6 + 1, 0, 0)),
            pl.BlockSpec((1, 1, D_MODEL), lambda bi, i, n: (bi * 6 + 0, 0, 0)),
            pl.BlockSpec((D_MODEL, tn), lambda bi, i, n: (0, n)),
        ],
        out_specs=pl.BlockSpec((1, tm, tn), lambda bi, i, n: (bi, i, n)),
        scratch_shapes=[pltpu.VMEM((tm, D_MODEL), BF16)],
        compiler_params=_params(("parallel", "parallel", "arbitrary")),
        name="in_proj",
    )(x, g_mix, ada3, ada3, w_in)


DA_NCH = 4


def _alibi_lane_constants():
    csum = np.zeros((DA_HEADS,), np.float32)
    aq = np.zeros((DA_HEADS, DA_HEAD_DIM, 1), np.float32)
    ak = np.zeros((DA_HEADS, 1, DA_HEAD_DIM), np.float32)
    for h in range(DA_HEADS):
        slope = 2.0 ** (-ALIBI_MAX_BIAS * (h + 1) / DA_HEADS)
        rem = slope * math.sqrt(DA_HEAD_DIM)
        pieces = []
        for _ in range(DA_NCH):
            p = float(np.float32(rem).astype(ml_dtypes.bfloat16).astype(np.float32))
            pieces.append(p)
            rem -= p
        csum[h] = np.float32(sum(pieces))
        for d, weight in enumerate((256.0, 16.0, 1.0)):
            for n, p in enumerate(pieces):
                ak[h, 0, d * DA_NCH + n] = -weight * p
                aq[h, (3 + d) * DA_NCH + n, 0] = p
    return csum, aq, ak


def _da_kernel(csum_ref, lam_ref, aq_ref, ak_ref, q_ref, k_ref, v_ref, g_ref, o_ref,
               acc_ref, qt_ref, ka_ref, vt_ref, corr_ref, s_ref, *, tk):
    h = pl.program_id(1)
    i = pl.program_id(2)
    tq = q_ref.shape[1]
    s_len = k_ref.shape[1]
    scale = DA_HEAD_DIM ** -0.5
    n3 = 3 * DA_NCH

    lq = lam_ref[...]
    e1 = jnp.exp(jnp.sum(lq[0:1] * lq[1:2], axis=-1, keepdims=True))
    e2 = jnp.exp(jnp.sum(lq[2:3] * lq[3:4], axis=-1, keepdims=True))
    lam = e1 - e2 + LAMBDA_INIT

    @pl.when(i == 0)
    def _():
        ck = ak_ref[0]
        lane = lax.broadcasted_iota(jnp.int32, (tk, DA_HEAD_DIM), 1)
        row = lax.broadcasted_iota(jnp.int32, (tk, DA_HEAD_DIM), 0)

        def build(r, carry):
            r0 = pl.multiple_of(r * tk, tk)
            pos = row + r * tk
            l2 = lane - n3
            dig = jnp.where(l2 < DA_NCH, (pos >> 8) << 8,
                            jnp.where(l2 < 2 * DA_NCH, ((pos >> 4) & 15) << 4, pos & 15)).astype(F32)
            kpos = jnp.where((lane >= n3) & (lane < 2 * n3), dig, ck).astype(BF16)
            for c in range(2):
                ka_ref[c, pl.ds(r0, tk), 0:DA_HEAD_DIM] = k_ref[0, pl.ds(r0, tk), c * DA_HEAD_DIM:(c + 1) * DA_HEAD_DIM]
                ka_ref[c, pl.ds(r0, tk), DA_HEAD_DIM:] = kpos
            vt_ref[r] = v_ref[0, pl.ds(r0, tk), :].astype(F32).T.astype(BF16)
            return carry

        lax.fori_loop(0, s_len // tk, build, 0)
        gap = (lax.broadcasted_iota(jnp.int32, (tk, tq), 0)
               - lax.broadcasted_iota(jnp.int32, (tk, tq), 1))
        for d in range(2):
            corr_ref[d] = jnp.maximum(gap + d * tk, 0).astype(F32) * (-2.0 * csum_ref[h])

    prow = lax.broadcasted_iota(jnp.int32, (DA_HEAD_DIM, tq), 0)
    pos = lax.broadcasted_iota(jnp.int32, (DA_HEAD_DIM, tq), 1) + i * tq
    dig = jnp.where(prow < DA_NCH, pos >> 8, jnp.where(prow < 2 * DA_NCH, (pos >> 4) & 15, pos & 15)).astype(F32)
    qpos = jnp.where(prow < n3, dig, aq_ref[0])
    for c in range(2):
        qct = q_ref[0, :, c * DA_HEAD_DIM:(c + 1) * DA_HEAD_DIM].astype(F32).T.astype(BF16)
        for sign in range(2):
            qt_ref[sign, c, 0:DA_HEAD_DIM, :] = qct
            qt_ref[sign, c, DA_HEAD_DIM:, :] = (qpos if sign == 0 else -qpos).astype(BF16)
    acc_ref[...] = jnp.zeros_like(acc_ref)

    c2 = scale * math.log2(math.e)

    def scores(j, slot, above, diag=None):
        j0 = pl.multiple_of(j * tk, tk)
        maxes = []
        for c in range(2):
            s = jnp.dot(ka_ref[c, pl.ds(j0, tk), :], qt_ref[above, c], preferred_element_type=F32)
            if diag is not None:
                s = s + corr_ref[diag]
            s_ref[slot, c] = s
            maxes.append(jnp.max(s, axis=0, keepdims=True))
        return tuple(maxes)

    def consume(j, slot, maxes, stats):
        vt = vt_ref[j]
        out = []
        for c in range(2):
            m, l = stats[2 * c], stats[2 * c + 1]
            m_new = jnp.maximum(m, maxes[c])
            alpha = jnp.exp2((m - m_new) * c2)
            p = jnp.exp2((s_ref[slot, c] - m_new) * c2)
            l_new = alpha * l + jnp.sum(p, axis=0, keepdims=True)
            acc_ref[c] = alpha * acc_ref[c] + jnp.dot(vt, p.astype(BF16), preferred_element_type=F32)
            out += [m_new, l_new]
        return tuple(out)

    nk = s_len // tk

    def off_diagonal(u):
        above = (u >= 2 * i).astype(jnp.int32)
        return u + 2 * above, above

    m_init = jnp.full((1, tq), NEG, F32)
    l_init = jnp.zeros((1, tq), F32)
    mx_d0 = scores(2 * i, 0, 0, diag=0)
    mx_d1 = scores(2 * i + 1, 1, 0, diag=1)
    stats = consume(2 * i, 0, mx_d0, (m_init, l_init, m_init, l_init))

    def pair(p, carry):
        j_prev, mx_prev, stats = carry
        j_a, above = off_diagonal(2 * p)
        mx_a = scores(j_a, 0, above)
        stats = consume(j_prev, 1, mx_prev, stats)
        j_b, above = off_diagonal(2 * p + 1)
        mx_b = scores(j_b, 1, above)
        stats = consume(j_a, 0, mx_a, stats)
        return j_b, mx_b, stats

    trips = jnp.where(i >= 0, (nk - 2) // 2, 0)
    j_last, mx_last, stats = lax.fori_loop(0, trips, pair, (2 * i + 1, mx_d1, stats))
    m0, l0, m1, l1 = consume(j_last, 1, mx_last, stats)

    ot = acc_ref[0] / l0 - lam * (acc_ref[1] / l1)
    ms = jnp.mean(ot * ot, axis=0, keepdims=True)
    y = (ot * lax.rsqrt(ms + EPS)).T * g_ref[...] * (1.0 - LAMBDA_INIT)
    o_ref[0] = y.astype(BF16)


def _diff_attention(proj, lam4, g_subln, tq=1024):
    b, s, _ = proj.shape
    tk = tq // 2
    assert s % tq == 0 and s >= 2 * tq and s <= 16 ** 3
    qb, kb, vb = COL_QA // DA_VDIM, COL_KA // DA_VDIM, COL_VA // DA_VDIM
    csum, aq, ak = _alibi_lane_constants()
    return pl.pallas_call(
        functools.partial(_da_kernel, tk=tk),
        out_shape=jax.ShapeDtypeStruct((b, s, DA_WIDTH), BF16),
        grid=(b, DA_HEADS, s // tq),
        in_specs=[
            pl.BlockSpec(memory_space=pltpu.SMEM),
            pl.BlockSpec((4, DA_HEAD_DIM), lambda bi, h, i: (0, 0)),
            pl.BlockSpec((1, DA_HEAD_DIM, 1), lambda bi, h, i: (h, 0, 0)),
            pl.BlockSpec((1, 1, DA_HEAD_DIM), lambda bi, h, i: (h, 0, 0)),
            pl.BlockSpec((1, tq, DA_VDIM), lambda bi, h, i: (bi, i, qb + h)),
            pl.BlockSpec((1, s, DA_VDIM), lambda bi, h, i: (bi, 0, kb + h)),
            pl.BlockSpec((1, s, DA_VDIM), lambda bi, h, i: (bi, 0, vb + h)),
            pl.BlockSpec((1, DA_VDIM), lambda bi, h, i: (0, 0)),
        ],
        out_specs=pl.BlockSpec((1, tq, DA_VDIM), lambda bi, h, i: (bi, i, h)),
        scratch_shapes=[
            pltpu.VMEM((2, DA_VDIM, tq), F32),
            pltpu.VMEM((2, 2, 2 * DA_HEAD_DIM, tq), BF16),
            pltpu.VMEM((2, s, 2 * DA_HEAD_DIM), BF16),
            pltpu.VMEM((s // tk, DA_VDIM, tk), BF16),
            pltpu.VMEM((2, tk, tq), F32),
            pltpu.VMEM((2, 2, tk, tq), F32),
        ],
        compiler_params=_params(("parallel", "parallel", "arbitrary")),
        name="diff_attn",
    )(jnp.asarray(csum), lam4, jnp.asarray(aq), jnp.asarray(ak), proj, proj, proj, g_subln)


def _na_variant_rows(variant, ri, kr):
    if variant == 0:
        valid, dr = 0 <= kr < NA_ROWS, kr - ri + NA_ROWS - 1
    elif variant == 1:
        valid, dr = ri <= kr < ri + NA_ROWS, kr - ri + NA_ROWS - 1 - NA_GROUP
    else:
        valid, dr = NA_GROUP <= kr < NA_KROWS, kr - ri - 1
    return dr if valid else None


def _na_bias_kernel(rpb_ref, tab_ref, tt_ref):
    h = pl.program_id(0)
    shape = (GRID_W, 2 * GRID_W)
    lane = lax.broadcasted_iota(jnp.int32, shape, 1)
    kcol = lax.broadcasted_iota(jnp.int32, shape, 0)
    col = lane & (GRID_W - 1)
    rel = kcol - col + (NA_COLS - 1)
    cstart = jnp.clip(col - NA_COLS // 2, 0, GRID_W - NA_COLS)
    band = (kcol >= cstart) & (kcol < cstart + NA_COLS)
    neg = jnp.full(shape, NEG, F32)
    for dr in range(NA_RPB_R):
        t = neg
        for dc in range(NA_RPB_C):
            val = rpb_ref[h * (NA_RPB_R * NA_RPB_C) + dr * NA_RPB_C + dc]
            t = jnp.where(rel == dc, val, t)
        tt_ref[dr] = jnp.where(band, t * math.log2(math.e), neg)
    first_half = lane < GRID_W
    for variant in range(3):
        for kr in range(NA_KROWS):
            for pair in range(NA_GROUP // 2):
                da = _na_variant_rows(variant, 2 * pair, kr)
                db = _na_variant_rows(variant, 2 * pair + 1, kr)
                ta = neg if da is None else tt_ref[da]
                tb = neg if db is None else tt_ref[db]
                tab_ref[0, variant, kr * GRID_W:(kr + 1) * GRID_W,
                        pair * 2 * GRID_W:(pair + 1) * 2 * GRID_W] = jnp.where(first_half, ta, tb)


def _na_bias(rpb_flat):
    return pl.pallas_call(
        _na_bias_kernel,
        out_shape=jax.ShapeDtypeStruct((NA_HEADS, 3, NA_K, NA_Q), F32),
        grid=(NA_HEADS,),
        in_specs=[pl.BlockSpec(memory_space=pltpu.SMEM)],
        out_specs=pl.BlockSpec((1, 3, NA_K, NA_Q), lambda h: (h, 0, 0, 0)),
        scratch_shapes=[pltpu.VMEM((NA_RPB_R, GRID_W, 2 * GRID_W), F32)],
        compiler_params=_params(("arbitrary",)),
        name="na_bias",
    )(rpb_flat)


def _na_kernel(q_ref, k_ref, v_ref, tab_ref, o_ref, vt_ref, s_ref, *, n_rows):
    n_groups = n_rows // NA_GROUP
    n_blocks = NA_KROWS // NA_GROUP
    c2 = (NA_HEAD_DIM ** -0.5) * math.log2(math.e)

    def transpose_values(g):
        g = jnp.minimum(g, n_groups - 1)
        r0 = pl.multiple_of(g * NA_Q, NA_Q)
        vt_ref[g] = v_ref[0, pl.ds(r0, NA_Q), :].astype(F32).T.astype(BF16)

    def window(g):
        return jnp.clip(g - 1, 0, n_groups - n_blocks)

    def scores(g, variant, slot):
        start = pl.multiple_of(window(g) * NA_Q, NA_Q)
        q0 = pl.multiple_of(g * NA_Q, NA_Q)
        kw = k_ref[0, pl.ds(start, NA_K), :]
        s = lax.dot_general(kw, q_ref[0, pl.ds(q0, NA_Q), :], (((1,), (1,)), ((), ())),
                            preferred_element_type=F32)
        s = s * c2 + tab_ref[0, variant]
        s_ref[slot] = s
        return jnp.max(s, axis=0, keepdims=True)

    def finish(g, slot, m):
        kb = window(g)
        q0 = pl.multiple_of(g * NA_Q, NA_Q)
        p = jnp.exp2(s_ref[slot] - m)
        l = jnp.sum(p, axis=0, keepdims=True)
        pb = p.astype(BF16)
        ot = jnp.dot(vt_ref[kb], pb[0:NA_Q], preferred_element_type=F32)
        for t in range(1, n_blocks):
            ot += jnp.dot(vt_ref[kb + t], pb[t * NA_Q:(t + 1) * NA_Q], preferred_element_type=F32)
        o_ref[0, pl.ds(q0, NA_Q), :] = (ot / l).T.astype(BF16)

    for g in range(n_blocks):
        transpose_values(g)
    m_first = scores(0, 0, 0)

    def pair(t, carry):
        g_prev, m_prev = carry
        g_a = 1 + 2 * t
        transpose_values(g_a + 2)
        transpose_values(g_a + 3)
        m_a = scores(g_a, 1, 1)
        finish(g_prev, 0, m_prev)
        g_b = g_a + 1
        m_b = scores(g_b, 1, 0)
        finish(g_a, 1, m_a)
        return g_b, m_b

    g_prev, m_prev = lax.fori_loop(0, (n_groups - 2) // 2, pair, (0, m_first))
    m_last = scores(n_groups - 1, 2, 1)
    finish(g_prev, 0, m_prev)
    finish(n_groups - 1, 1, m_last)


def _neighborhood_attention(proj, table):
    b, s, _ = proj.shape
    n_rows = s // GRID_W
    assert n_rows % NA_GROUP == 0 and n_rows >= NA_KROWS and NA_ROWS // 2 == NA_GROUP
    assert NA_KROWS % NA_GROUP == 0 and (n_rows // NA_GROUP) % 2 == 0
    qb, kb, vb = COL_QN // NA_HEAD_DIM, COL_KN // NA_HEAD_DIM, COL_VN // NA_HEAD_DIM
    return pl.pallas_call(
        functools.partial(_na_kernel, n_rows=n_rows),
        out_shape=jax.ShapeDtypeStruct((b, s, NA_WIDTH), BF16),
        grid=(NA_HEADS, b),
        in_specs=[
            pl.BlockSpec((1, s, NA_HEAD_DIM), lambda h, bi: (bi, 0, qb + h)),
            pl.BlockSpec((1, s, NA_HEAD_DIM), lambda h, bi: (bi, 0, kb + h)),
            pl.BlockSpec((1, s, NA_HEAD_DIM), lambda h, bi: (bi, 0, vb + h)),
            pl.BlockSpec((1, 3, NA_K, NA_Q), lambda h, bi: (h, 0, 0, 0)),
        ],
        out_specs=pl.BlockSpec((1, s, NA_HEAD_DIM), lambda h, bi: (bi, 0, h)),
        scratch_shapes=[pltpu.VMEM((n_rows // NA_GROUP, NA_HEAD_DIM, NA_Q), BF16),
                        pltpu.VMEM((2, NA_K, NA_Q), F32)],
        compiler_params=_params(("parallel", "parallel")),
        name="nbr_attn",
    )(proj, proj, proj, table)


def _sigmoid(x):
    return 1.0 / (1.0 + jnp.exp(-x))


def _merge_kernel(x_ref, oa_ref, on_ref, ga_ref, gb_ref, wpa_ref, wpb_ref, wout_ref,
                  gt1_ref, gmlp_ref, sc2_ref, sh2_ref, x1_ref, h2_ref):
    pa = jnp.dot(oa_ref[0], wpa_ref[...], preferred_element_type=F32)
    pb = jnp.dot(on_ref[0], wpb_ref[...], preferred_element_type=F32)
    merged = _sigmoid(ga_ref[0].astype(F32)) * pa + _sigmoid(gb_ref[0].astype(F32)) * pb
    out = jnp.dot(merged.astype(BF16), wout_ref[...], preferred_element_type=F32)
    x1 = x_ref[0] + gt1_ref[0] * out
    x1_ref[0] = x1
    ms = jnp.mean(x1 * x1, axis=-1, keepdims=True)
    y = x1 * lax.rsqrt(ms + EPS) * gmlp_ref[...]
    h2_ref[0] = (y * (1.0 + sc2_ref[0]) + sh2_ref[0]).astype(BF16)


def _merge(x, oa, on, proj, w_pa, w_pb, w_out, ada3, g_mlp, tm=512):
    b, s, _ = x.shape
    ga_b, gb_b = COL_GA // D_MODEL, COL_GB // D_MODEL
    once = pl.Buffered(1)

    def ada_spec(idx):
        return pl.BlockSpec((1, 1, D_MODEL), lambda bi, i: (bi * 6 + idx, 0, 0))

    return pl.pallas_call(
        _merge_kernel,
        out_shape=(jax.ShapeDtypeStruct((b, s, D_MODEL), F32),
                   jax.ShapeDtypeStruct((b, s, D_MODEL), BF16)),
        grid=(b, s // tm),
        in_specs=[
            pl.BlockSpec((1, tm, D_MODEL), lambda bi, i: (bi, i, 0)),
            pl.BlockSpec((1, tm, DA_WIDTH), lambda bi, i: (bi, i, 0)),
            pl.BlockSpec((1, tm, NA_WIDTH), lambda bi, i: (bi, i, 0)),
            pl.BlockSpec((1, tm, D_MODEL), lambda bi, i: (bi, i, ga_b)),
            pl.BlockSpec((1, tm, D_MODEL), lambda bi, i: (bi, i, gb_b)),
            pl.BlockSpec((DA_WIDTH, D_MODEL), lambda bi, i: (0, 0), pipeline_mode=once),
            pl.BlockSpec((NA_WIDTH, D_MODEL), lambda bi, i: (0, 0), pipeline_mode=once),
            pl.BlockSpec((D_MODEL, D_MODEL), lambda bi, i: (0, 0), pipeline_mode=once),
            ada_spec(2),
            pl.BlockSpec((1, D_MODEL), lambda bi, i: (0, 0)),
            ada_spec(4),
            ada_spec(3),
        ],
        out_specs=(pl.BlockSpec((1, tm, D_MODEL), lambda bi, i: (bi, i, 0)),
                   pl.BlockSpec((1, tm, D_MODEL), lambda bi, i: (bi, i, 0))),
        compiler_params=_params(("parallel", "parallel")),
        name="merge_out_proj",
    )(x, oa, on, proj, proj, w_pa, w_pb, w_out, ada3, g_mlp, ada3, ada3)


def _mlp_kernel(h2_ref, w1_ref, w2_ref, x1_ref, gt2_ref, gf_ref, y_ref):
    f = pl.program_id(2)

    @pl.when(f == 0)
    def _():
        y_ref[...] = jnp.zeros_like(y_ref)

    u = jnp.dot(h2_ref[0], w1_ref[...], preferred_element_type=F32)
    u = jnp.square(jnp.maximum(u, 0.0)).astype(BF16)
    y_ref[0] += jnp.dot(u, w2_ref[...], preferred_element_type=F32)

    @pl.when(f == pl.num_programs(2) - 1)
    def _():
        x2 = x1_ref[0] + gt2_ref[0] * y_ref[0]
        ms = jnp.mean(x2 * x2, axis=-1, keepdims=True)
        y_ref[0] = x2 * lax.rsqrt(ms + EPS) * gf_ref[...]


def _mlp(h2, x1, w1, w2, ada3, g_final, tm=512, tf=1024):
    b, s, _ = x1.shape
    return pl.pallas_call(
        _mlp_kernel,
        out_shape=jax.ShapeDtypeStruct((b, s, D_MODEL), F32),
        grid=(b, s // tm, D_FF // tf),
        in_specs=[
            pl.BlockSpec((1, tm, D_MODEL), lambda bi, i, f: (bi, i, 0)),
            pl.BlockSpec((D_MODEL, tf), lambda bi, i, f: (0, f)),
            pl.BlockSpec((tf, D_MODEL), lambda bi, i, f: (f, 0)),
            pl.BlockSpec((1, tm, D_MODEL), lambda bi, i, f: (bi, i, 0)),
            pl.BlockSpec((1, 1, D_MODEL), lambda bi, i, f: (bi * 6 + 5, 0, 0)),
            pl.BlockSpec((1, D_MODEL), lambda bi, i, f: (0, 0)),
        ],
        out_specs=pl.BlockSpec((1, tm, D_MODEL), lambda bi, i, f: (bi, i, 0)),
        compiler_params=_params(("parallel", "parallel", "arbitrary")),
        name="mlp_final_norm",
    )(h2, w1, w2, x1, ada3, g_final)


def _trunk(x, ada, w, table, lam4):
    b = x.shape[0]
    ada3 = ada.reshape(b * 6, 1, D_MODEL)
    proj = _inproj(x, ada3, w["g_mix"], w["w_in"])
    oa = _diff_attention(proj, lam4, w["g_subln"])
    on = _neighborhood_attention(proj, table)
    x1, h2 = _merge(x, oa, on, proj, w["w_pa"], w["w_pb"], w["w_out"], ada3, w["g_mlp"])
    return _mlp(h2, x1, w["w1"], w["w2"], ada3, w["g_final"])


def kernel(x_prompt, x_sample, c_prompt, c_sample, w_ada, b_ada, g_mix, w_in, lam_q1, lam_k1, lam_q2, lam_k2, g_subln, rpb, w_pa, w_pb, w_out, g_mlp, w1, w2, g_final):
    w = {
        "g_mix": g_mix[0].reshape(1, D_MODEL),
        "w_in": w_in[0].astype(BF16),
        "g_subln": g_subln[0].reshape(1, DA_VDIM),
        "w_pa": w_pa[0].astype(BF16),
        "w_pb": w_pb[0].astype(BF16),
        "w_out": w_out[0].astype(BF16),
        "g_mlp": g_mlp[0].reshape(1, D_MODEL),
        "w1": w1[0].astype(BF16),
        "w2": w2[0].astype(BF16),
        "g_final": g_final.reshape(1, D_MODEL),
    }
    nb_p = c_prompt.shape[0]
    ada = _ada(jnp.concatenate([c_prompt, c_sample], axis=0), w_ada[0], b_ada[0])
    table = _na_bias(rpb[0].reshape(-1))
    lam4 = jnp.stack([lam_q1[0], lam_k1[0], lam_q2[0], lam_k2[0]], axis=0)
    y_prompt = _trunk(x_prompt, ada[:nb_p], w, table, lam4)
    y_sample = _trunk(x_sample, ada[nb_p:], w, table, lam4)
    return (y_prompt, y_sample)
```

```python
import functools
import math

import jax
import jax.numpy as jnp
import ml_dtypes
import numpy as np
from jax import lax
from jax.experimental import pallas as pl
from jax.experimental.pallas import tpu as pltpu

F32 = jnp.float32
BF16 = jnp.bfloat16

D_MODEL = 2048
DA_HEADS = 4
DA_HEAD_DIM = 128
DA_VDIM = 256
DA_WIDTH = DA_HEADS * DA_VDIM
NA_HEADS = 8
NA_HEAD_DIM = 128
NA_WIDTH = NA_HEADS * NA_HEAD_DIM
GRID_W = 64
NA_ROWS = 8
NA_COLS = 16
D_FF = 4 * D_MODEL
ALIBI_MAX_BIAS = 8.0
EPS = 1e-6
IN_W = 3 * DA_WIDTH + 3 * NA_WIDTH + 2 * D_MODEL
LAMBDA_INIT = 0.8 - 0.6 * math.exp(-0.3 * 0)

COL_QA, COL_KA, COL_VA = 0, DA_WIDTH, 2 * DA_WIDTH
COL_QN = 3 * DA_WIDTH
COL_KN = COL_QN + NA_WIDTH
COL_VN = COL_KN + NA_WIDTH
COL_GA = COL_VN + NA_WIDTH
COL_GB = COL_GA + D_MODEL

NA_GROUP = 4
NA_KROWS = NA_GROUP + NA_ROWS
NA_Q = NA_GROUP * GRID_W
NA_K = NA_KROWS * GRID_W
NA_RPB_R = 2 * NA_ROWS - 1
NA_RPB_C = 2 * NA_COLS - 1
NEG = -1e30

VMEM_LIMIT = 56 * 1024 * 1024


def _params(sem):
    return pltpu.CompilerParams(dimension_semantics=sem, vmem_limit_bytes=VMEM_LIMIT)


def _ada_kernel(c_ref, w_ref, b_ref, o_ref):
    c = c_ref[...].astype(BF16)
    w = w_ref[...].astype(BF16)
    o_ref[...] = jnp.dot(c, w, preferred_element_type=F32) + b_ref[...]


def _ada(c_all, w_ada, b_ada, tn=1024):
    nb = c_all.shape[0]
    n = w_ada.shape[1]
    return pl.pallas_call(
        _ada_kernel,
        out_shape=jax.ShapeDtypeStruct((nb, n), F32),
        grid=(n // tn,),
        in_specs=[
            pl.BlockSpec((nb, D_MODEL), lambda j: (0, 0)),
            pl.BlockSpec((D_MODEL, tn), lambda j: (0, j)),
            pl.BlockSpec((1, tn), lambda j: (0, j)),
        ],
        out_specs=pl.BlockSpec((nb, tn), lambda j: (0, j)),
        compiler_params=_params(("parallel",)),
        name="ada_proj",
    )(c_all, w_ada, b_ada.reshape(1, n))


def _modulated_norm(x, g, sc, sh):
    ms = jnp.mean(x * x, axis=-1, keepdims=True)
    return ((x * lax.rsqrt(ms + EPS) * g) * (1.0 + sc) + sh).astype(BF16)


def _inproj_kernel(x_ref, g_ref, sc_ref, sh_ref, w_ref, o_ref, h_ref, *, rows):
    @pl.when(pl.program_id(2) == 0)
    def _():
        for c in range(x_ref.shape[1] // rows):
            sl = pl.ds(c * rows, rows)
            h = _modulated_norm(x_ref[0, sl, :], g_ref[...], sc_ref[0], sh_ref[0])
            h_ref[sl, :] = h
            o_ref[0, sl, :] = jnp.dot(h, w_ref[...], preferred_element_type=F32).astype(BF16)

    @pl.when(pl.program_id(2) != 0)
    def _():
        o_ref[0] = jnp.dot(h_ref[...], w_ref[...], preferred_element_type=F32).astype(BF16)


def _inproj(x, ada3, g_mix, w_in, tm=1024, tn=2048, rows=256):
    b, s, _ = x.shape
    return pl.pallas_call(
        functools.partial(_inproj_kernel, rows=rows),
        out_shape=jax.ShapeDtypeStruct((b, s, IN_W), BF16),
        grid=(b, s // tm, IN_W // tn),
        in_specs=[
            pl.BlockSpec((1, tm, D_MODEL), lambda bi, i, n: (bi, i, 0)),
            pl.BlockSpec((1, D_MODEL), lambda bi, i, n: (0, 0)),
            pl.BlockSpec((1, 1, D_MODEL), lambda bi, i, n: (bi * 6 + 1, 0, 0)),
            pl.BlockSpec((1, 1, D_MODEL), lambda bi, i, n: (bi * 6 + 0, 0, 0)),
            pl.BlockSpec((D_MODEL, tn), lambda bi, i, n: (0, n)),
        ],
        out_specs=pl.BlockSpec((1, tm, tn), lambda bi, i, n: (bi, i, n)),
        scratch_shapes=[pltpu.VMEM((tm, D_MODEL), BF16)],
        compiler_params=_params(("parallel", "parallel", "arbitrary")),
        name="in_proj",
    )(x, g_mix, ada3, ada3, w_in)


DA_NCH = 4


def _alibi_lane_constants():
    csum = np.zeros((DA_HEADS,), np.float32)
    aq = np.zeros((DA_HEADS, DA_HEAD_DIM, 1), np.float32)
    ak = np.zeros((DA_HEADS, 1, DA_HEAD_DIM), np.float32)
    for h in range(DA_HEADS):
        slope = 2.0 ** (-ALIBI_MAX_BIAS * (h + 1) / DA_HEADS)
        rem = slope * math.sqrt(DA_HEAD_DIM)
        pieces = []
        for _ in range(DA_NCH):
            p = float(np.float32(rem).astype(ml_dtypes.bfloat16).astype(np.float32))
            pieces.append(p)
            rem -= p
        csum[h] = np.float32(sum(pieces))
        for d, weight in enumerate((256.0, 16.0, 1.0)):
            for n, p in enumerate(pieces):
                ak[h, 0, d * DA_NCH + n] = -weight * p
                aq[h, (3 + d) * DA_NCH + n, 0] = p
    return csum, aq, ak


def _da_kernel(csum_ref, lam_ref, aq_ref, ak_ref, q_ref, k_ref, v_ref, g_ref, o_ref,
               acc_ref, qt_ref, ka_ref, vt_ref, corr_ref, s_ref, *, tk):
    h = pl.program_id(1)
    i = pl.program_id(2)
    tq = q_ref.shape[1]
    s_len = k_ref.shape[1]
    scale = DA_HEAD_DIM ** -0.5
    n3 = 3 * DA_NCH

    lq = lam_ref[...]
    e1 = jnp.exp(jnp.sum(lq[0:1] * lq[1:2], axis=-1, keepdims=True))
    e2 = jnp.exp(jnp.sum(lq[2:3] * lq[3:4], axis=-1, keepdims=True))
    lam = e1 - e2 + LAMBDA_INIT

    @pl.when(i == 0)
    def _():
        ck = ak_ref[0]
        lane = lax.broadcasted_iota(jnp.int32, (tk, DA_HEAD_DIM), 1)
        row = lax.broadcasted_iota(jnp.int32, (tk, DA_HEAD_DIM), 0)

        def build(r, carry):
            r0 = pl.multiple_of(r * tk, tk)
            pos = row + r * tk
            l2 = lane - n3
            dig = jnp.where(l2 < DA_NCH, (pos >> 8) << 8,
                            jnp.where(l2 < 2 * DA_NCH, ((pos >> 4) & 15) << 4, pos & 15)).astype(F32)
            kpos = jnp.where((lane >= n3) & (lane < 2 * n3), dig, ck).astype(BF16)
            for c in range(2):
                ka_ref[c, pl.ds(r0, tk), 0:DA_HEAD_DIM] = k_ref[0, pl.ds(r0, tk), c * DA_HEAD_DIM:(c + 1) * DA_HEAD_DIM]
                ka_ref[c, pl.ds(r0, tk), DA_HEAD_DIM:] = kpos
            vt_ref[r] = v_ref[0, pl.ds(r0, tk), :].astype(F32).T.astype(BF16)
            return carry

        lax.fori_loop(0, s_len // tk, build, 0)
        gap = (lax.broadcasted_iota(jnp.int32, (tk, tq), 0)
               - lax.broadcasted_iota(jnp.int32, (tk, tq), 1))
        for d in range(2):
            corr_ref[d] = jnp.maximum(gap + d * tk, 0).astype(F32) * (-2.0 * csum_ref[h])

    prow = lax.broadcasted_iota(jnp.int32, (DA_HEAD_DIM, tq), 0)
    pos = lax.broadcasted_iota(jnp.int32, (DA_HEAD_DIM, tq), 1) + i * tq
    dig = jnp.where(prow < DA_NCH, pos >> 8, jnp.where(prow < 2 * DA_NCH, (pos >> 4) & 15, pos & 15)).astype(F32)
    qpos = jnp.where(prow < n3, dig, aq_ref[0])
    for c in range(2):
        qct = q_ref[0, :, c * DA_HEAD_DIM:(c + 1) * DA_HEAD_DIM].astype(F32).T.astype(BF16)
        for sign in range(2):
            qt_ref[sign, c, 0:DA_HEAD_DIM, :] = qct
            qt_ref[sign, c, DA_HEAD_DIM:, :] = (qpos if sign == 0 else -qpos).astype(BF16)
    acc_ref[...] = jnp.zeros_like(acc_ref)

    c2 = scale * math.log2(math.e)

    def scores(j, slot, above, diag=None):
        j0 = pl.multiple_of(j * tk, tk)
        maxes = []
        for c in range(2):
            s = jnp.dot(ka_ref[c, pl.ds(j0, tk), :], qt_ref[above, c], preferred_element_type=F32)
            if diag is not None:
                s = s + corr_ref[diag]
            s_ref[slot, c] = s
            maxes.append(jnp.max(s, axis=0, keepdims=True))
        return tuple(maxes)

    def consume(j, slot, maxes, stats):
        vt = vt_ref[j]
        out = []
        for c in range(2):
            m, l = stats[2 * c], stats[2 * c + 1]
            m_new = jnp.maximum(m, maxes[c])
            alpha = jnp.exp2((m - m_new) * c2)
            p = jnp.exp2((s_ref[slot, c] - m_new) * c2)
            l_new = alpha * l + jnp.sum(p, axis=0, keepdims=True)
            acc_ref[c] = alpha * acc_ref[c] + jnp.dot(vt, p.astype(BF16), preferred_element_type=F32)
            out += [m_new, l_new]
        return tuple(out)

    nk = s_len // tk

    def off_diagonal(u):
        above = (u >= 2 * i).astype(jnp.int32)
        return u + 2 * above, above

    m_init = jnp.full((1, tq), NEG, F32)
    l_init = jnp.zeros((1, tq), F32)
    mx_d0 = scores(2 * i, 0, 0, diag=0)
    mx_d1 = scores(2 * i + 1, 1, 0, diag=1)
    stats = consume(2 * i, 0, mx_d0, (m_init, l_init, m_init, l_init))

    def pair(p, carry):
        j_prev, mx_prev, stats = carry
        j_a, above = off_diagonal(2 * p)
        mx_a = scores(j_a, 0, above)
        stats = consume(j_prev, 1, mx_prev, stats)
        j_b, above = off_diagonal(2 * p + 1)
        mx_b = scores(j_b, 1, above)
        stats = consume(j_a, 0, mx_a, stats)
        return j_b, mx_b, stats

    trips = jnp.where(i >= 0, (nk - 2) // 2, 0)
    j_last, mx_last, stats = lax.fori_loop(0, trips, pair, (2 * i + 1, mx_d1, stats))
    m0, l0, m1, l1 = consume(j_last, 1, mx_last, stats)

    ot = acc_ref[0] / l0 - lam * (acc_ref[1] / l1)
    ms = jnp.mean(ot * ot, axis=0, keepdims=True)
    y = (ot * lax.rsqrt(ms + EPS)).T * g_ref[...] * (1.0 - LAMBDA_INIT)
    o_ref[0] = y.astype(BF16)


def _diff_attention(proj, lam4, g_subln, tq=1024):
    b, s, _ = proj.shape
    tk = tq // 2
    assert s % tq == 0 and s >= 2 * tq and s <= 16 ** 3
    qb, kb, vb = COL_QA // DA_VDIM, COL_KA // DA_VDIM, COL_VA // DA_VDIM
    csum, aq, ak = _alibi_lane_constants()
    return pl.pallas_call(
        functools.partial(_da_kernel, tk=tk),
        out_shape=jax.ShapeDtypeStruct((b, s, DA_WIDTH), BF16),
        grid=(b, DA_HEADS, s // tq),
        in_specs=[
            pl.BlockSpec(memory_space=pltpu.SMEM),
            pl.BlockSpec((4, DA_HEAD_DIM), lambda bi, h, i: (0, 0)),
            pl.BlockSpec((1, DA_HEAD_DIM, 1), lambda bi, h, i: (h, 0, 0)),
            pl.BlockSpec((1, 1, DA_HEAD_DIM), lambda bi, h, i: (h, 0, 0)),
            pl.BlockSpec((1, tq, DA_VDIM), lambda bi, h, i: (bi, i, qb + h)),
            pl.BlockSpec((1, s, DA_VDIM), lambda bi, h, i: (bi, 0, kb + h)),
            pl.BlockSpec((1, s, DA_VDIM), lambda bi, h, i: (bi, 0, vb + h)),
            pl.BlockSpec((1, DA_VDIM), lambda bi, h, i: (0, 0)),
        ],
        out_specs=pl.BlockSpec((1, tq, DA_VDIM), lambda bi, h, i: (bi, i, h)),
        scratch_shapes=[
            pltpu.VMEM((2, DA_VDIM, tq), F32),
            pltpu.VMEM((2, 2, 2 * DA_HEAD_DIM, tq), BF16),
            pltpu.VMEM((2, s, 2 * DA_HEAD_DIM), BF16),
            pltpu.VMEM((s // tk, DA_VDIM, tk), BF16),
            pltpu.VMEM((2, tk, tq), F32),
            pltpu.VMEM((2, 2, tk, tq), F32),
        ],
        compiler_params=_params(("parallel", "parallel", "arbitrary")),
        name="diff_attn",
    )(jnp.asarray(csum), lam4, jnp.asarray(aq), jnp.asarray(ak), proj, proj, proj, g_subln)


def _na_variant_rows(variant, ri, kr):
    if variant == 0:
        valid, dr = 0 <= kr < NA_ROWS, kr - ri + NA_ROWS - 1
    elif variant == 1:
        valid, dr = ri <= kr < ri + NA_ROWS, kr - ri + NA_ROWS - 1 - NA_GROUP
    else:
        valid, dr = NA_GROUP <= kr < NA_KROWS, kr - ri - 1
    return dr if valid else None


def _na_bias_kernel(rpb_ref, tab_ref, tt_ref):
    h = pl.program_id(0)
    shape = (GRID_W, 2 * GRID_W)
    lane = lax.broadcasted_iota(jnp.int32, shape, 1)
    kcol = lax.broadcasted_iota(jnp.int32, shape, 0)
    col = lane & (GRID_W - 1)
    rel = kcol - col + (NA_COLS - 1)
    cstart = jnp.clip(col - NA_COLS // 2, 0, GRID_W - NA_COLS)
    band = (kcol >= cstart) & (kcol < cstart + NA_COLS)
    neg = jnp.full(shape, NEG, F32)
    for dr in range(NA_RPB_R):
        t = neg
        for dc in range(NA_RPB_C):
            val = rpb_ref[h * (NA_RPB_R * NA_RPB_C) + dr * NA_RPB_C + dc]
            t = jnp.where(rel == dc, val, t)
        tt_ref[dr] = jnp.where(band, t * math.log2(math.e), neg)
    first_half = lane < GRID_W
    for variant in range(3):
        for kr in range(NA_KROWS):
            for pair in range(NA_GROUP // 2):
                da = _na_variant_rows(variant, 2 * pair, kr)
                db = _na_variant_rows(variant, 2 * pair + 1, kr)
                ta = neg if da is None else tt_ref[da]
                tb = neg if db is None else tt_ref[db]
                tab_ref[0, variant, kr * GRID_W:(kr + 1) * GRID_W,
                        pair * 2 * GRID_W:(pair + 1) * 2 * GRID_W] = jnp.where(first_half, ta, tb)


def _na_bias(rpb_flat):
    return pl.pallas_call(
        _na_bias_kernel,
        out_shape=jax.ShapeDtypeStruct((NA_HEADS, 3, NA_K, NA_Q), F32),
        grid=(NA_HEADS,),
        in_specs=[pl.BlockSpec(memory_space=pltpu.SMEM)],
        out_specs=pl.BlockSpec((1, 3, NA_K, NA_Q), lambda h: (h, 0, 0, 0)),
        scratch_shapes=[pltpu.VMEM((NA_RPB_R, GRID_W, 2 * GRID_W), F32)],
        compiler_params=_params(("arbitrary",)),
        name="na_bias",
    )(rpb_flat)


def _na_kernel(q_ref, k_ref, v_ref, tab_ref, o_ref, vt_ref, s_ref, *, n_rows):
    n_groups = n_rows // NA_GROUP
    n_blocks = NA_KROWS // NA_GROUP
    c2 = (NA_HEAD_DIM ** -0.5) * math.log2(math.e)

    heads = range(q_ref.shape[2] // NA_HEAD_DIM)

    def lanes(hh):
        return slice(hh * NA_HEAD_DIM, (hh + 1) * NA_HEAD_DIM)

    def transpose_values(g):
        g = jnp.minimum(g, n_groups - 1)
        r0 = pl.multiple_of(g * NA_Q, NA_Q)
        for hh in heads:
            vt_ref[hh, g] = v_ref[0, pl.ds(r0, NA_Q), lanes(hh)].astype(F32).T.astype(BF16)

    def window(g):
        return jnp.clip(g - 1, 0, n_groups - n_blocks)

    def scores(g, variant, slot):
        start = pl.multiple_of(window(g) * NA_Q, NA_Q)
        q0 = pl.multiple_of(g * NA_Q, NA_Q)
        maxes = []
        for hh in heads:
            kw = k_ref[0, pl.ds(start, NA_K), lanes(hh)]
            s = lax.dot_general(kw, q_ref[0, pl.ds(q0, NA_Q), lanes(hh)], (((1,), (1,)), ((), ())),
                                preferred_element_type=F32)
            s = s * c2 + tab_ref[hh, variant]
            s_ref[hh, slot] = s
            maxes.append(jnp.max(s, axis=0, keepdims=True))
        return tuple(maxes)

    def finish(g, slot, maxes):
        kb = window(g)
        q0 = pl.multiple_of(g * NA_Q, NA_Q)
        for hh in heads:
            p = jnp.exp2(s_ref[hh, slot] - maxes[hh])
            l = jnp.sum(p, axis=0, keepdims=True)
            pb = p.astype(BF16)
            ot = jnp.dot(vt_ref[hh, kb], pb[0:NA_Q], preferred_element_type=F32)
            for t in range(1, n_blocks):
                ot += jnp.dot(vt_ref[hh, kb + t], pb[t * NA_Q:(t + 1) * NA_Q], preferred_element_type=F32)
            o_ref[0, pl.ds(q0, NA_Q), lanes(hh)] = (ot / l).T.astype(BF16)

    for g in range(n_blocks):
        transpose_values(g)
    m_first = scores(0, 0, 0)

    def pair(t, carry):
        g_prev, m_prev = carry
        g_a = 1 + 2 * t
        transpose_values(g_a + 2)
        transpose_values(g_a + 3)
        m_a = scores(g_a, 1, 1)
        finish(g_prev, 0, m_prev)
        g_b = g_a + 1
        m_b = scores(g_b, 1, 0)
        finish(g_a, 1, m_a)
        return g_b, m_b

    g_prev, m_prev = lax.fori_loop(0, (n_groups - 2) // 2, pair, (0, m_first))
    m_last = scores(n_groups - 1, 2, 1)
    finish(g_prev, 0, m_prev)
    finish(n_groups - 1, 1, m_last)


def _neighborhood_attention(proj, table):
    b, s, _ = proj.shape
    n_rows = s // GRID_W
    assert n_rows % NA_GROUP == 0 and n_rows >= NA_KROWS and NA_ROWS // 2 == NA_GROUP
    assert NA_KROWS % NA_GROUP == 0 and (n_rows // NA_GROUP) % 2 == 0
    hpb = 2
    width = hpb * NA_HEAD_DIM
    assert NA_HEADS % hpb == 0
    qb, kb, vb = COL_QN // width, COL_KN // width, COL_VN // width
    return pl.pallas_call(
        functools.partial(_na_kernel, n_rows=n_rows),
        out_shape=jax.ShapeDtypeStruct((b, s, NA_WIDTH), BF16),
        grid=(NA_HEADS // hpb, b),
        in_specs=[
            pl.BlockSpec((1, s, width), lambda h, bi: (bi, 0, qb + h)),
            pl.BlockSpec((1, s, width), lambda h, bi: (bi, 0, kb + h)),
            pl.BlockSpec((1, s, width), lambda h, bi: (bi, 0, vb + h)),
            pl.BlockSpec((hpb, 3, NA_K, NA_Q), lambda h, bi: (h, 0, 0, 0)),
        ],
        out_specs=pl.BlockSpec((1, s, width), lambda h, bi: (bi, 0, h)),
        scratch_shapes=[pltpu.VMEM((hpb, n_rows // NA_GROUP, NA_HEAD_DIM, NA_Q), BF16),
                        pltpu.VMEM((hpb, 2, NA_K, NA_Q), F32)],
        compiler_params=_params(("parallel", "parallel")),
        name="nbr_attn",
    )(proj, proj, proj, table)


def _sigmoid(x):
    return 1.0 / (1.0 + jnp.exp(-x))


def _merge_kernel(x_ref, oa_ref, on_ref, ga_ref, gb_ref, wpa_ref, wpb_ref, wout_ref,
                  gt1_ref, gmlp_ref, sc2_ref, sh2_ref, x1_ref, h2_ref):
    pa = jnp.dot(oa_ref[0], wpa_ref[...], preferred_element_type=F32)
    pb = jnp.dot(on_ref[0], wpb_ref[...], preferred_element_type=F32)
    merged = _sigmoid(ga_ref[0].astype(F32)) * pa + _sigmoid(gb_ref[0].astype(F32)) * pb
    out = jnp.dot(merged.astype(BF16), wout_ref[...], preferred_element_type=F32)
    x1 = x_ref[0] + gt1_ref[0] * out
    x1_ref[0] = x1
    ms = jnp.mean(x1 * x1, axis=-1, keepdims=True)
    y = x1 * lax.rsqrt(ms + EPS) * gmlp_ref[...]
    h2_ref[0] = (y * (1.0 + sc2_ref[0]) + sh2_ref[0]).astype(BF16)


def _merge(x, oa, on, proj, w_pa, w_pb, w_out, ada3, g_mlp, tm=512):
    b, s, _ = x.shape
    ga_b, gb_b = COL_GA // D_MODEL, COL_GB // D_MODEL
    once = pl.Buffered(1)

    def ada_spec(idx):
        return pl.BlockSpec((1, 1, D_MODEL), lambda bi, i: (bi * 6 + idx, 0, 0))

    return pl.pallas_call(
        _merge_kernel,
        out_shape=(jax.ShapeDtypeStruct((b, s, D_MODEL), F32),
                   jax.ShapeDtypeStruct((b, s, D_MODEL), BF16)),
        grid=(b, s // tm),
        in_specs=[
            pl.BlockSpec((1, tm, D_MODEL), lambda bi, i: (bi, i, 0)),
            pl.BlockSpec((1, tm, DA_WIDTH), lambda bi, i: (bi, i, 0)),
            pl.BlockSpec((1, tm, NA_WIDTH), lambda bi, i: (bi, i, 0)),
            pl.BlockSpec((1, tm, D_MODEL), lambda bi, i: (bi, i, ga_b)),
            pl.BlockSpec((1, tm, D_MODEL), lambda bi, i: (bi, i, gb_b)),
            pl.BlockSpec((DA_WIDTH, D_MODEL), lambda bi, i: (0, 0), pipeline_mode=once),
            pl.BlockSpec((NA_WIDTH, D_MODEL), lambda bi, i: (0, 0), pipeline_mode=once),
            pl.BlockSpec((D_MODEL, D_MODEL), lambda bi, i: (0, 0), pipeline_mode=once),
            ada_spec(2),
            pl.BlockSpec((1, D_MODEL), lambda bi, i: (0, 0)),
            ada_spec(4),
            ada_spec(3),
        ],
        out_specs=(pl.BlockSpec((1, tm, D_MODEL), lambda bi, i: (bi, i, 0)),
                   pl.BlockSpec((1, tm, D_MODEL), lambda bi, i: (bi, i, 0))),
        compiler_params=_params(("parallel", "parallel")),
        name="merge_out_proj",
    )(x, oa, on, proj, proj, w_pa, w_pb, w_out, ada3, g_mlp, ada3, ada3)


def _mlp_kernel(h2_ref, w1_ref, w2_ref, x1_ref, gt2_ref, gf_ref, y_ref, *, rows):
    f = pl.program_id(2)
    last = pl.num_programs(2) - 1

    def contribution(sl):
        u = jnp.dot(h2_ref[0, sl, :], w1_ref[...], preferred_element_type=F32)
        u = jnp.square(jnp.maximum(u, 0.0)).astype(BF16)
        return jnp.dot(u, w2_ref[...], preferred_element_type=F32)

    @pl.when(f == 0)
    def _():
        y_ref[0] = contribution(slice(None))

    @pl.when((f != 0) & (f != last))
    def _():
        y_ref[0] += contribution(slice(None))

    @pl.when(f == last)
    def _():
        for c in range(y_ref.shape[1] // rows):
            sl = pl.ds(c * rows, rows)
            x2 = x1_ref[0, sl, :] + gt2_ref[0] * (y_ref[0, sl, :] + contribution(sl))
            ms = jnp.mean(x2 * x2, axis=-1, keepdims=True)
            y_ref[0, sl, :] = x2 * lax.rsqrt(ms + EPS) * gf_ref[...]


def _mlp(h2, x1, w1, w2, ada3, g_final, tm=512, tf=1024):
    b, s, _ = x1.shape
    return pl.pallas_call(
        functools.partial(_mlp_kernel, rows=tm // 2),
        out_shape=jax.ShapeDtypeStruct((b, s, D_MODEL), F32),
        grid=(b, s // tm, D_FF // tf),
        in_specs=[
            pl.BlockSpec((1, tm, D_MODEL), lambda bi, i, f: (bi, i, 0)),
            pl.BlockSpec((D_MODEL, tf), lambda bi, i, f: (0, f)),
            pl.BlockSpec((tf, D_MODEL), lambda bi, i, f: (f, 0)),
            pl.BlockSpec((1, tm, D_MODEL), lambda bi, i, f: (bi, i, 0)),
            pl.BlockSpec((1, 1, D_MODEL), lambda bi, i, f: (bi * 6 + 5, 0, 0)),
            pl.BlockSpec((1, D_MODEL), lambda bi, i, f: (0, 0)),
        ],
        out_specs=pl.BlockSpec((1, tm, D_MODEL), lambda bi, i, f: (bi, i, 0)),
        compiler_params=_params(("parallel", "parallel", "arbitrary")),
        name="mlp_final_norm",
    )(h2, w1, w2, x1, ada3, g_final)


def _trunk(x, ada, w, table, lam4):
    b = x.shape[0]
    ada3 = ada.reshape(b * 6, 1, D_MODEL)
    proj = _inproj(x, ada3, w["g_mix"], w["w_in"])
    oa = _diff_attention(proj, lam4, w["g_subln"])
    on = _neighborhood_attention(proj, table)
    x1, h2 = _merge(x, oa, on, proj, w["w_pa"], w["w_pb"], w["w_out"], ada3, w["g_mlp"])
    return _mlp(h2, x1, w["w1"], w["w2"], ada3, w["g_final"])


def kernel(x_prompt, x_sample, c_prompt, c_sample, w_ada, b_ada, g_mix, w_in, lam_q1, lam_k1, lam_q2, lam_k2, g_subln, rpb, w_pa, w_pb, w_out, g_mlp, w1, w2, g_final):
    w = {
        "g_mix": g_mix[0].reshape(1, D_MODEL),
        "w_in": w_in[0].astype(BF16),
        "g_subln": g_subln[0].reshape(1, DA_VDIM),
        "w_pa": w_pa[0].astype(BF16),
        "w_pb": w_pb[0].astype(BF16),
        "w_out": w_out[0].astype(BF16),
        "g_mlp": g_mlp[0].reshape(1, D_MODEL),
        "w1": w1[0].astype(BF16),
        "w2": w2[0].astype(BF16),
        "g_final": g_final.reshape(1, D_MODEL),
    }
    nb_p = c_prompt.shape[0]
    ada = _ada(jnp.concatenate([c_prompt, c_sample], axis=0), w_ada[0], b_ada[0])
    table = _na_bias(rpb[0].reshape(-1))
    lam4 = jnp.stack([lam_q1[0], lam_k1[0], lam_q2[0], lam_k2[0]], axis=0)
    y_prompt = _trunk(x_prompt, ada[:nb_p], w, table, lam4)
    y_sample = _trunk(x_sample, ada[nb_p:], w, table, lam4)
    return (y_prompt, y_sample)
```

```python
import functools
import math

import jax
import jax.numpy as jnp
import ml_dtypes
import numpy as np
from jax import lax
from jax.experimental import pallas as pl
from jax.experimental.pallas import tpu as pltpu

F32 = jnp.float32
BF16 = jnp.bfloat16

D_MODEL = 2048
DA_HEADS = 4
DA_HEAD_DIM = 128
DA_VDIM = 256
DA_WIDTH = DA_HEADS * DA_VDIM
NA_HEADS = 8
NA_HEAD_DIM = 128
NA_WIDTH = NA_HEADS * NA_HEAD_DIM
GRID_W = 64
NA_ROWS = 8
NA_COLS = 16
D_FF = 4 * D_MODEL
ALIBI_MAX_BIAS = 8.0
EPS = 1e-6
IN_W = 3 * DA_WIDTH + 3 * NA_WIDTH + 2 * D_MODEL
LAMBDA_INIT = 0.8 - 0.6 * math.exp(-0.3 * 0)

COL_QA, COL_KA, COL_VA = 0, DA_WIDTH, 2 * DA_WIDTH
COL_QN = 3 * DA_WIDTH
COL_KN = COL_QN + NA_WIDTH
COL_VN = COL_KN + NA_WIDTH
COL_GA = COL_VN + NA_WIDTH
COL_GB = COL_GA + D_MODEL

NA_GROUP = 4
NA_KROWS = NA_GROUP + NA_ROWS
NA_Q = NA_GROUP * GRID_W
NA_K = NA_KROWS * GRID_W
NA_RPB_R = 2 * NA_ROWS - 1
NA_RPB_C = 2 * NA_COLS - 1
NEG = -1e30
ONES_ROWS = 16

VMEM_LIMIT = 56 * 1024 * 1024


def _params(sem):
    return pltpu.CompilerParams(dimension_semantics=sem, vmem_limit_bytes=VMEM_LIMIT)


def _ada_kernel(c_ref, w_ref, b_ref, o_ref):
    c = c_ref[...].astype(BF16)
    w = w_ref[...].astype(BF16)
    o_ref[...] = jnp.dot(c, w, preferred_element_type=F32) + b_ref[...]


def _ada(c_all, w_ada, b_ada, tn=1024):
    nb = c_all.shape[0]
    n = w_ada.shape[1]
    return pl.pallas_call(
        _ada_kernel,
        out_shape=jax.ShapeDtypeStruct((nb, n), F32),
        grid=(n // tn,),
        in_specs=[
            pl.BlockSpec((nb, D_MODEL), lambda j: (0, 0)),
            pl.BlockSpec((D_MODEL, tn), lambda j: (0, j)),
            pl.BlockSpec((1, tn), lambda j: (0, j)),
        ],
        out_specs=pl.BlockSpec((nb, tn), lambda j: (0, j)),
        compiler_params=_params(("parallel",)),
        name="ada_proj",
    )(c_all, w_ada, b_ada.reshape(1, n))


def _modulated_norm(x, g, sc, sh):
    ms = jnp.mean(x * x, axis=-1, keepdims=True)
    return ((x * lax.rsqrt(ms + EPS) * g) * (1.0 + sc) + sh).astype(BF16)


def _inproj_kernel(x_ref, g_ref, sc_ref, sh_ref, w_ref, cs_ref, o_ref, h_ref, *, rows):
    @pl.when(pl.program_id(2) == 0)
    def _():
        for c in range(x_ref.shape[1] // rows):
            sl = pl.ds(c * rows, rows)
            h = _modulated_norm(x_ref[0, sl, :], g_ref[...], sc_ref[0], sh_ref[0])
            h_ref[sl, :] = h
            o_ref[0, sl, :] = (jnp.dot(h, w_ref[...], preferred_element_type=F32) * cs_ref[...]).astype(BF16)

    @pl.when(pl.program_id(2) != 0)
    def _():
        o_ref[0] = (jnp.dot(h_ref[...], w_ref[...], preferred_element_type=F32) * cs_ref[...]).astype(BF16)


QUERY_SCALE = (DA_HEAD_DIM ** -0.5) * math.log2(math.e)


def _inproj(x, ada3, g_mix, w_in, tm=1024, tn=2048, rows=256):
    b, s, _ = x.shape
    assert DA_HEAD_DIM == NA_HEAD_DIM
    col_scale = np.ones((1, IN_W), np.float32)
    col_scale[0, COL_QA:COL_QA + DA_WIDTH] = QUERY_SCALE
    col_scale[0, COL_QN:COL_QN + NA_WIDTH] = QUERY_SCALE
    return pl.pallas_call(
        functools.partial(_inproj_kernel, rows=rows),
        out_shape=jax.ShapeDtypeStruct((b, s, IN_W), BF16),
        grid=(b, s // tm, IN_W // tn),
        in_specs=[
            pl.BlockSpec((1, tm, D_MODEL), lambda bi, i, n: (bi, i, 0)),
            pl.BlockSpec((1, D_MODEL), lambda bi, i, n: (0, 0)),
            pl.BlockSpec((1, 1, D_MODEL), lambda bi, i, n: (bi * 6 + 1, 0, 0)),
            pl.BlockSpec((1, 1, D_MODEL), lambda bi, i, n: (bi * 6 + 0, 0, 0)),
            pl.BlockSpec((D_MODEL, tn), lambda bi, i, n: (0, n)),
            pl.BlockSpec((1, tn), lambda bi, i, n: (0, n)),
        ],
        out_specs=pl.BlockSpec((1, tm, tn), lambda bi, i, n: (bi, i, n)),
        scratch_shapes=[pltpu.VMEM((tm, D_MODEL), BF16)],
        compiler_params=_params(("parallel", "parallel", "arbitrary")),
        name="in_proj",
    )(x, g_mix, ada3, ada3, w_in, jnp.asarray(col_scale))


DA_NCH = 4


def _alibi_lane_constants():
    csum = np.zeros((DA_HEADS,), np.float32)
    aq = np.zeros((DA_HEADS, DA_HEAD_DIM, 1), np.float32)
    ak = np.zeros((DA_HEADS, 1, DA_HEAD_DIM), np.float32)
    for h in range(DA_HEADS):
        slope = 2.0 ** (-ALIBI_MAX_BIAS * (h + 1) / DA_HEADS)
        rem = slope * math.log2(math.e)
        pieces = []
        for _ in range(DA_NCH):
            p = float(np.float32(rem).astype(ml_dtypes.bfloat16).astype(np.float32))
            pieces.append(p)
            rem -= p
        csum[h] = np.float32(sum(pieces))
        for d, weight in enumerate((256.0, 16.0, 1.0)):
            for n, p in enumerate(pieces):
                ak[h, 0, d * DA_NCH + n] = -weight * p
                aq[h, (3 + d) * DA_NCH + n, 0] = p
    return csum, aq, ak


def _da_kernel(csum_ref, lam_ref, aq_ref, ak_ref, q_ref, k_ref, v_ref, g_ref, o_ref,
               acc_ref, qt_ref, ka_ref, vt_ref, corr_ref, s_ref, *, tk):
    h = pl.program_id(1)
    i = pl.program_id(2)
    tq = q_ref.shape[1]
    s_len = k_ref.shape[1]
    n3 = 3 * DA_NCH

    lq = lam_ref[...]
    e1 = jnp.exp(jnp.sum(lq[0:1] * lq[1:2], axis=-1, keepdims=True))
    e2 = jnp.exp(jnp.sum(lq[2:3] * lq[3:4], axis=-1, keepdims=True))
    lam = e1 - e2 + LAMBDA_INIT

    @pl.when(i == 0)
    def _():
        ck = ak_ref[0]
        lane = lax.broadcasted_iota(jnp.int32, (tk, DA_HEAD_DIM), 1)
        row = lax.broadcasted_iota(jnp.int32, (tk, DA_HEAD_DIM), 0)

        def build(r, carry):
            r0 = pl.multiple_of(r * tk, tk)
            pos = row + r * tk
            l2 = lane - n3
            dig = jnp.where(l2 < DA_NCH, (pos >> 8) << 8,
                            jnp.where(l2 < 2 * DA_NCH, ((pos >> 4) & 15) << 4, pos & 15)).astype(F32)
            kpos = jnp.where((lane >= n3) & (lane < 2 * n3), dig, ck).astype(BF16)
            for c in range(2):
                ka_ref[c, pl.ds(r0, tk), 0:DA_HEAD_DIM] = k_ref[0, pl.ds(r0, tk), c * DA_HEAD_DIM:(c + 1) * DA_HEAD_DIM]
                ka_ref[c, pl.ds(r0, tk), DA_HEAD_DIM:] = kpos
            vt_ref[r] = v_ref[0, pl.ds(r0, tk), :].astype(F32).T.astype(BF16)
            return carry

        lax.fori_loop(0, s_len // tk, build, 0)
        gap = (lax.broadcasted_iota(jnp.int32, (tk, tq), 0)
               - lax.broadcasted_iota(jnp.int32, (tk, tq), 1))
        for d in range(2):
            corr_ref[d] = jnp.maximum(gap + d * tk, 0).astype(F32) * (-2.0 * csum_ref[h])

    prow = lax.broadcasted_iota(jnp.int32, (DA_HEAD_DIM, tq), 0)
    pos = lax.broadcasted_iota(jnp.int32, (DA_HEAD_DIM, tq), 1) + i * tq
    dig = jnp.where(prow < DA_NCH, pos >> 8, jnp.where(prow < 2 * DA_NCH, (pos >> 4) & 15, pos & 15)).astype(F32)
    qpos = jnp.where(prow < n3, dig, aq_ref[0])
    for c in range(2):
        qct = q_ref[0, :, c * DA_HEAD_DIM:(c + 1) * DA_HEAD_DIM].astype(F32).T.astype(BF16)
        for sign in range(2):
            qt_ref[sign, c, 0:DA_HEAD_DIM, :] = qct
            qt_ref[sign, c, DA_HEAD_DIM:, :] = (qpos if sign == 0 else -qpos).astype(BF16)
    acc_ref[...] = jnp.zeros_like(acc_ref)

    def scores(j, slot, above, diag=None):
        j0 = pl.multiple_of(j * tk, tk)
        maxes = []
        for c in range(2):
            s = jnp.dot(ka_ref[c, pl.ds(j0, tk), :], qt_ref[above, c], preferred_element_type=F32)
            if diag is not None:
                s = s + corr_ref[diag]
            s_ref[slot, c] = s
            maxes.append(jnp.max(s, axis=0, keepdims=True))
        return tuple(maxes)

    def consume(j, slot, maxes, stats):
        vt = vt_ref[j]
        out = []
        for c in range(2):
            m, l = stats[2 * c], stats[2 * c + 1]
            m_new = jnp.maximum(m, maxes[c])
            alpha = jnp.exp2(m - m_new)
            p = jnp.exp2(s_ref[slot, c] - m_new)
            l_new = alpha * l + jnp.sum(p, axis=0, keepdims=True)
            acc_ref[c] = alpha * acc_ref[c] + jnp.dot(vt, p.astype(BF16), preferred_element_type=F32)
            out += [m_new, l_new]
        return tuple(out)

    nk = s_len // tk

    def off_diagonal(u):
        above = (u >= 2 * i).astype(jnp.int32)
        return u + 2 * above, above

    m_init = jnp.full((1, tq), NEG, F32)
    l_init = jnp.zeros((1, tq), F32)
    mx_d0 = scores(2 * i, 0, 0, diag=0)
    mx_d1 = scores(2 * i + 1, 1, 0, diag=1)
    stats = consume(2 * i, 0, mx_d0, (m_init, l_init, m_init, l_init))

    def pair(p, carry):
        j_prev, mx_prev, stats = carry
        j_a, above = off_diagonal(2 * p)
        mx_a = scores(j_a, 0, above)
        stats = consume(j_prev, 1, mx_prev, stats)
        j_b, above = off_diagonal(2 * p + 1)
        mx_b = scores(j_b, 1, above)
        stats = consume(j_a, 0, mx_a, stats)
        return j_b, mx_b, stats

    trips = jnp.where(i >= 0, (nk - 2) // 2, 0)
    j_last, mx_last, stats = lax.fori_loop(0, trips, pair, (2 * i + 1, mx_d1, stats))
    m0, l0, m1, l1 = consume(j_last, 1, mx_last, stats)

    ot = acc_ref[0] / l0 - lam * (acc_ref[1] / l1)
    ms = jnp.mean(ot * ot, axis=0, keepdims=True)
    y = (ot * lax.rsqrt(ms + EPS)).T * g_ref[...] * (1.0 - LAMBDA_INIT)
    o_ref[0] = y.astype(BF16)


def _diff_attention(proj, lam4, g_subln, tq=1024):
    b, s, _ = proj.shape
    tk = tq // 2
    assert s % tq == 0 and s >= 2 * tq and s <= 16 ** 3
    qb, kb, vb = COL_QA // DA_VDIM, COL_KA // DA_VDIM, COL_VA // DA_VDIM
    csum, aq, ak = _alibi_lane_constants()
    return pl.pallas_call(
        functools.partial(_da_kernel, tk=tk),
        out_shape=jax.ShapeDtypeStruct((b, s, DA_WIDTH), BF16),
        grid=(b, DA_HEADS, s // tq),
        in_specs=[
            pl.BlockSpec(memory_space=pltpu.SMEM),
            pl.BlockSpec((4, DA_HEAD_DIM), lambda bi, h, i: (0, 0)),
            pl.BlockSpec((1, DA_HEAD_DIM, 1), lambda bi, h, i: (h, 0, 0)),
            pl.BlockSpec((1, 1, DA_HEAD_DIM), lambda bi, h, i: (h, 0, 0)),
            pl.BlockSpec((1, tq, DA_VDIM), lambda bi, h, i: (bi, i, qb + h)),
            pl.BlockSpec((1, s, DA_VDIM), lambda bi, h, i: (bi, 0, kb + h)),
            pl.BlockSpec((1, s, DA_VDIM), lambda bi, h, i: (bi, 0, vb + h)),
            pl.BlockSpec((1, DA_VDIM), lambda bi, h, i: (0, 0)),
        ],
        out_specs=pl.BlockSpec((1, tq, DA_VDIM), lambda bi, h, i: (bi, i, h)),
        scratch_shapes=[
            pltpu.VMEM((2, DA_VDIM, tq), F32),
            pltpu.VMEM((2, 2, 2 * DA_HEAD_DIM, tq), BF16),
            pltpu.VMEM((2, s, 2 * DA_HEAD_DIM), BF16),
            pltpu.VMEM((s // tk, DA_VDIM, tk), BF16),
            pltpu.VMEM((2, tk, tq), F32),
            pltpu.VMEM((2, 2, tk, tq), F32),
        ],
        compiler_params=_params(("parallel", "parallel", "arbitrary")),
        name="diff_attn",
    )(jnp.asarray(csum), lam4, jnp.asarray(aq), jnp.asarray(ak), proj, proj, proj, g_subln)


def _na_variant_rows(variant, ri, kr):
    if variant == 0:
        valid, dr = 0 <= kr < NA_ROWS, kr - ri + NA_ROWS - 1
    elif variant == 1:
        valid, dr = ri <= kr < ri + NA_ROWS, kr - ri + NA_ROWS - 1 - NA_GROUP
    else:
        valid, dr = NA_GROUP <= kr < NA_KROWS, kr - ri - 1
    return dr if valid else None


def _na_bias_kernel(rpb_ref, tab_ref, tt_ref):
    h = pl.program_id(0)
    shape = (GRID_W, 2 * GRID_W)
    lane = lax.broadcasted_iota(jnp.int32, shape, 1)
    kcol = lax.broadcasted_iota(jnp.int32, shape, 0)
    col = lane & (GRID_W - 1)
    rel = kcol - col + (NA_COLS - 1)
    cstart = jnp.clip(col - NA_COLS // 2, 0, GRID_W - NA_COLS)
    band = (kcol >= cstart) & (kcol < cstart + NA_COLS)
    neg = jnp.full(shape, NEG, F32)
    for dr in range(NA_RPB_R):
        t = neg
        for dc in range(NA_RPB_C):
            val = rpb_ref[h * (NA_RPB_R * NA_RPB_C) + dr * NA_RPB_C + dc]
            t = jnp.where(rel == dc, val, t)
        tt_ref[dr] = jnp.where(band, t * math.log2(math.e), neg)
    first_half = lane < GRID_W
    for variant in range(3):
        for kr in range(NA_KROWS):
            for pair in range(NA_GROUP // 2):
                da = _na_variant_rows(variant, 2 * pair, kr)
                db = _na_variant_rows(variant, 2 * pair + 1, kr)
                ta = neg if da is None else tt_ref[da]
                tb = neg if db is None else tt_ref[db]
                tab_ref[0, variant, kr * GRID_W:(kr + 1) * GRID_W,
                        pair * 2 * GRID_W:(pair + 1) * 2 * GRID_W] = jnp.where(first_half, ta, tb)


def _na_bias(rpb_flat):
    return pl.pallas_call(
        _na_bias_kernel,
        out_shape=jax.ShapeDtypeStruct((NA_HEADS, 3, NA_K, NA_Q), F32),
        grid=(NA_HEADS,),
        in_specs=[pl.BlockSpec(memory_space=pltpu.SMEM)],
        out_specs=pl.BlockSpec((1, 3, NA_K, NA_Q), lambda h: (h, 0, 0, 0)),
        scratch_shapes=[pltpu.VMEM((NA_RPB_R, GRID_W, 2 * GRID_W), F32)],
        compiler_params=_params(("arbitrary",)),
        name="na_bias",
    )(rpb_flat)


def _na_kernel(q_ref, k_ref, v_ref, tab_ref, o_ref, vt_ref, s_ref, *, n_rows):
    n_groups = n_rows // NA_GROUP
    n_blocks = NA_KROWS // NA_GROUP

    heads = range(q_ref.shape[2] // NA_HEAD_DIM)

    def lanes(hh):
        return slice(hh * NA_HEAD_DIM, (hh + 1) * NA_HEAD_DIM)

    def transpose_values(g):
        g = jnp.minimum(g, n_groups - 1)
        r0 = pl.multiple_of(g * NA_Q, NA_Q)
        for hh in heads:
            vt_ref[hh, g, 0:NA_HEAD_DIM, :] = v_ref[0, pl.ds(r0, NA_Q), lanes(hh)].astype(F32).T.astype(BF16)
            vt_ref[hh, g, NA_HEAD_DIM:, :] = jnp.ones((ONES_ROWS, NA_Q), BF16)

    def window(g):
        return jnp.clip(g - 1, 0, n_groups - n_blocks)

    def scores(g, variant, slot):
        start = pl.multiple_of(window(g) * NA_Q, NA_Q)
        q0 = pl.multiple_of(g * NA_Q, NA_Q)
        maxes = []
        for hh in heads:
            kw = k_ref[0, pl.ds(start, NA_K), lanes(hh)]
            s = lax.dot_general(kw, q_ref[0, pl.ds(q0, NA_Q), lanes(hh)], (((1,), (1,)), ((), ())),
                                preferred_element_type=F32)
            s = s + tab_ref[hh, variant]
            s_ref[hh, slot] = s
            maxes.append(jnp.max(s, axis=0, keepdims=True))
        return tuple(maxes)

    def finish(g, slot, maxes):
        kb = window(g)
        q0 = pl.multiple_of(g * NA_Q, NA_Q)
        for hh in heads:
            pb = jnp.exp2(s_ref[hh, slot] - maxes[hh]).astype(BF16)
            ot = jnp.dot(vt_ref[hh, kb], pb[0:NA_Q], preferred_element_type=F32)
            for t in range(1, n_blocks):
                ot += jnp.dot(vt_ref[hh, kb + t], pb[t * NA_Q:(t + 1) * NA_Q], preferred_element_type=F32)
            l = ot[NA_HEAD_DIM:NA_HEAD_DIM + 1, :]
            o_ref[0, pl.ds(q0, NA_Q), lanes(hh)] = (ot[0:NA_HEAD_DIM, :] / l).T.astype(BF16)

    for g in range(n_blocks):
        transpose_values(g)
    m_first = scores(0, 0, 0)

    def pair(t, carry):
        g_prev, m_prev = carry
        g_a = 1 + 2 * t
        transpose_values(g_a + 2)
        transpose_values(g_a + 3)
        m_a = scores(g_a, 1, 1)
        finish(g_prev, 0, m_prev)
        g_b = g_a + 1
        m_b = scores(g_b, 1, 0)
        finish(g_a, 1, m_a)
        return g_b, m_b

    g_prev, m_prev = lax.fori_loop(0, (n_groups - 2) // 2, pair, (0, m_first))
    m_last = scores(n_groups - 1, 2, 1)
    finish(g_prev, 0, m_prev)
    finish(n_groups - 1, 1, m_last)


def _neighborhood_attention(proj, table):
    b, s, _ = proj.shape
    n_rows = s // GRID_W
    assert n_rows % NA_GROUP == 0 and n_rows >= NA_KROWS and NA_ROWS // 2 == NA_GROUP
    assert NA_KROWS % NA_GROUP == 0 and (n_rows // NA_GROUP) % 2 == 0
    hpb = 2
    width = hpb * NA_HEAD_DIM
    assert NA_HEADS % hpb == 0
    qb, kb, vb = COL_QN // width, COL_KN // width, COL_VN // width
    return pl.pallas_call(
        functools.partial(_na_kernel, n_rows=n_rows),
        out_shape=jax.ShapeDtypeStruct((b, s, NA_WIDTH), BF16),
        grid=(NA_HEADS // hpb, b),
        in_specs=[
            pl.BlockSpec((1, s, width), lambda h, bi: (bi, 0, qb + h)),
            pl.BlockSpec((1, s, width), lambda h, bi: (bi, 0, kb + h)),
            pl.BlockSpec((1, s, width), lambda h, bi: (bi, 0, vb + h)),
            pl.BlockSpec((hpb, 3, NA_K, NA_Q), lambda h, bi: (h, 0, 0, 0)),
        ],
        out_specs=pl.BlockSpec((1, s, width), lambda h, bi: (bi, 0, h)),
        scratch_shapes=[pltpu.VMEM((hpb, n_rows // NA_GROUP, NA_HEAD_DIM + ONES_ROWS, NA_Q), BF16),
                        pltpu.VMEM((hpb, 2, NA_K, NA_Q), F32)],
        compiler_params=_params(("parallel", "parallel")),
        name="nbr_attn",
    )(proj, proj, proj, table)


def _sigmoid(x):
    return 1.0 / (1.0 + jnp.exp(-x))


def _merge_kernel(x_ref, oa_ref, on_ref, ga_ref, gb_ref, wpa_ref, wpb_ref, wout_ref,
                  gt1_ref, gmlp_ref, sc2_ref, sh2_ref, x1_ref, h2_ref):
    pa = jnp.dot(oa_ref[0], wpa_ref[...], preferred_element_type=F32)
    pb = jnp.dot(on_ref[0], wpb_ref[...], preferred_element_type=F32)
    merged = _sigmoid(ga_ref[0].astype(F32)) * pa + _sigmoid(gb_ref[0].astype(F32)) * pb
    out = jnp.dot(merged.astype(BF16), wout_ref[...], preferred_element_type=F32)
    x1 = x_ref[0] + gt1_ref[0] * out
    x1_ref[0] = x1
    ms = jnp.mean(x1 * x1, axis=-1, keepdims=True)
    y = x1 * lax.rsqrt(ms + EPS) * gmlp_ref[...]
    h2_ref[0] = (y * (1.0 + sc2_ref[0]) + sh2_ref[0]).astype(BF16)


def _merge(x, oa, on, proj, w_pa, w_pb, w_out, ada3, g_mlp, tm=512):
    b, s, _ = x.shape
    ga_b, gb_b = COL_GA // D_MODEL, COL_GB // D_MODEL
    once = pl.Buffered(1)

    def ada_spec(idx):
        return pl.BlockSpec((1, 1, D_MODEL), lambda bi, i: (bi * 6 + idx, 0, 0))

    return pl.pallas_call(
        _merge_kernel,
        out_shape=(jax.ShapeDtypeStruct((b, s, D_MODEL), F32),
                   jax.ShapeDtypeStruct((b, s, D_MODEL), BF16)),
        grid=(b, s // tm),
        in_specs=[
            pl.BlockSpec((1, tm, D_MODEL), lambda bi, i: (bi, i, 0)),
            pl.BlockSpec((1, tm, DA_WIDTH), lambda bi, i: (bi, i, 0)),
            pl.BlockSpec((1, tm, NA_WIDTH), lambda bi, i: (bi, i, 0)),
            pl.BlockSpec((1, tm, D_MODEL), lambda bi, i: (bi, i, ga_b)),
            pl.BlockSpec((1, tm, D_MODEL), lambda bi, i: (bi, i, gb_b)),
            pl.BlockSpec((DA_WIDTH, D_MODEL), lambda bi, i: (0, 0), pipeline_mode=once),
            pl.BlockSpec((NA_WIDTH, D_MODEL), lambda bi, i: (0, 0), pipeline_mode=once),
            pl.BlockSpec((D_MODEL, D_MODEL), lambda bi, i: (0, 0), pipeline_mode=once),
            ada_spec(2),
            pl.BlockSpec((1, D_MODEL), lambda bi, i: (0, 0)),
            ada_spec(4),
            ada_spec(3),
        ],
        out_specs=(pl.BlockSpec((1, tm, D_MODEL), lambda bi, i: (bi, i, 0)),
                   pl.BlockSpec((1, tm, D_MODEL), lambda bi, i: (bi, i, 0))),
        compiler_params=_params(("parallel", "parallel")),
        name="merge_out_proj",
    )(x, oa, on, proj, proj, w_pa, w_pb, w_out, ada3, g_mlp, ada3, ada3)


def _mlp_kernel(h2_ref, w1_ref, w2_ref, x1_ref, gt2_ref, gf_ref, y_ref, *, rows):
    f = pl.program_id(2)
    last = pl.num_programs(2) - 1

    def contribution(sl):
        u = jnp.dot(h2_ref[0, sl, :], w1_ref[...], preferred_element_type=F32)
        u = jnp.square(jnp.maximum(u, 0.0)).astype(BF16)
        return jnp.dot(u, w2_ref[...], preferred_element_type=F32)

    @pl.when(f == 0)
    def _():
        y_ref[0] = contribution(slice(None))

    @pl.when((f != 0) & (f != last))
    def _():
        y_ref[0] += contribution(slice(None))

    @pl.when(f == last)
    def _():
        for c in range(y_ref.shape[1] // rows):
            sl = pl.ds(c * rows, rows)
            x2 = x1_ref[0, sl, :] + gt2_ref[0] * (y_ref[0, sl, :] + contribution(sl))
            ms = jnp.mean(x2 * x2, axis=-1, keepdims=True)
            y_ref[0, sl, :] = x2 * lax.rsqrt(ms + EPS) * gf_ref[...]


def _mlp(h2, x1, w1, w2, ada3, g_final, tm=512, tf=1024):
    b, s, _ = x1.shape
    return pl.pallas_call(
        functools.partial(_mlp_kernel, rows=tm // 2),
        out_shape=jax.ShapeDtypeStruct((b, s, D_MODEL), F32),
        grid=(b, s // tm, D_FF // tf),
        in_specs=[
            pl.BlockSpec((1, tm, D_MODEL), lambda bi, i, f: (bi, i, 0)),
            pl.BlockSpec((D_MODEL, tf), lambda bi, i, f: (0, f)),
            pl.BlockSpec((tf, D_MODEL), lambda bi, i, f: (f, 0)),
            pl.BlockSpec((1, tm, D_MODEL), lambda bi, i, f: (bi, i, 0)),
            pl.BlockSpec((1, 1, D_MODEL), lambda bi, i, f: (bi * 6 + 5, 0, 0)),
            pl.BlockSpec((1, D_MODEL), lambda bi, i, f: (0, 0)),
        ],
        out_specs=pl.BlockSpec((1, tm, D_MODEL), lambda bi, i, f: (bi, i, 0)),
        compiler_params=_params(("parallel", "parallel", "arbitrary")),
        name="mlp_final_norm",
    )(h2, w1, w2, x1, ada3, g_final)


def _trunk(x, ada, w, table, lam4):
    b = x.shape[0]
    ada3 = ada.reshape(b * 6, 1, D_MODEL)
    proj = _inproj(x, ada3, w["g_mix"], w["w_in"])
    oa = _diff_attention(proj, lam4, w["g_subln"])
    on = _neighborhood_attention(proj, table)
    x1, h2 = _merge(x, oa, on, proj, w["w_pa"], w["w_pb"], w["w_out"], ada3, w["g_mlp"])
    return _mlp(h2, x1, w["w1"], w["w2"], ada3, w["g_final"])


def kernel(x_prompt, x_sample, c_prompt, c_sample, w_ada, b_ada, g_mix, w_in, lam_q1, lam_k1, lam_q2, lam_k2, g_subln, rpb, w_pa, w_pb, w_out, g_mlp, w1, w2, g_final):
    w = {
        "g_mix": g_mix[0].reshape(1, D_MODEL),
        "w_in": w_in[0].astype(BF16),
        "g_subln": g_subln[0].reshape(1, DA_VDIM),
        "w_pa": w_pa[0].astype(BF16),
        "w_pb": w_pb[0].astype(BF16),
        "w_out": w_out[0].astype(BF16),
        "g_mlp": g_mlp[0].reshape(1, D_MODEL),
        "w1": w1[0].astype(BF16),
        "w2": w2[0].astype(BF16),
        "g_final": g_final.reshape(1, D_MODEL),
    }
    nb_p = c_prompt.shape[0]
    ada = _ada(jnp.concatenate([c_prompt, c_sample], axis=0), w_ada[0], b_ada[0])
    table = _na_bias(rpb[0].reshape(-1))
    lam4 = jnp.stack([lam_q1[0], lam_k1[0], lam_q2[0], lam_k2[0]], axis=0)
    y_prompt = _trunk(x_prompt, ada[:nb_p], w, table, lam4)
    y_sample = _trunk(x_sample, ada[nb_p:], w, table, lam4)
    return (y_prompt, y_sample)
```

```python
import functools
import math

import jax
import jax.numpy as jnp
import ml_dtypes
import numpy as np
from jax import lax
from jax.experimental import pallas as pl
from jax.experimental.pallas import tpu as pltpu

F32 = jnp.float32
BF16 = jnp.bfloat16

D_MODEL = 2048
DA_HEADS = 4
DA_HEAD_DIM = 128
DA_VDIM = 256
DA_WIDTH = DA_HEADS * DA_VDIM
NA_HEADS = 8
NA_HEAD_DIM = 128
NA_WIDTH = NA_HEADS * NA_HEAD_DIM
GRID_W = 64
NA_ROWS = 8
NA_COLS = 16
D_FF = 4 * D_MODEL
ALIBI_MAX_BIAS = 8.0
EPS = 1e-6
IN_W = 3 * DA_WIDTH + 3 * NA_WIDTH + 2 * D_MODEL
LAMBDA_INIT = 0.8 - 0.6 * math.exp(-0.3 * 0)

COL_QA, COL_KA, COL_VA = 0, DA_WIDTH, 2 * DA_WIDTH
COL_QN = 3 * DA_WIDTH
COL_KN = COL_QN + NA_WIDTH
COL_VN = COL_KN + NA_WIDTH
COL_GA = COL_VN + NA_WIDTH
COL_GB = COL_GA + D_MODEL

NA_GROUP = 4
NA_KROWS = NA_GROUP + NA_ROWS
NA_Q = NA_GROUP * GRID_W
NA_K = NA_KROWS * GRID_W
NA_RPB_R = 2 * NA_ROWS - 1
NA_RPB_C = 2 * NA_COLS - 1
NEG = -1e30
ONES_ROWS = 16

VMEM_LIMIT = 56 * 1024 * 1024


def _params(sem):
    return pltpu.CompilerParams(dimension_semantics=sem, vmem_limit_bytes=VMEM_LIMIT)


def _ada_kernel(c_ref, w_ref, b_ref, o_ref):
    c = c_ref[...].astype(BF16)
    w = w_ref[...].astype(BF16)
    o_ref[...] = jnp.dot(c, w, preferred_element_type=F32) + b_ref[...]


def _ada(c_all, w_ada, b_ada, tn=1024):
    nb = c_all.shape[0]
    n = w_ada.shape[1]
    return pl.pallas_call(
        _ada_kernel,
        out_shape=jax.ShapeDtypeStruct((nb, n), F32),
        grid=(n // tn,),
        in_specs=[
            pl.BlockSpec((nb, D_MODEL), lambda j: (0, 0)),
            pl.BlockSpec((D_MODEL, tn), lambda j: (0, j)),
            pl.BlockSpec((1, tn), lambda j: (0, j)),
        ],
        out_specs=pl.BlockSpec((nb, tn), lambda j: (0, j)),
        compiler_params=_params(("parallel",)),
        name="ada_proj",
    )(c_all, w_ada, b_ada.reshape(1, n))


def _modulated_norm(x, g, sc, sh):
    ms = jnp.mean(x * x, axis=-1, keepdims=True)
    return ((x * lax.rsqrt(ms + EPS) * g) * (1.0 + sc) + sh).astype(BF16)


def _inproj_kernel(x_ref, g_ref, sc_ref, sh_ref, w_ref, cs_ref, o_ref, h_ref, *, rows):
    @pl.when(pl.program_id(2) == 0)
    def _():
        for c in range(x_ref.shape[1] // rows):
            sl = pl.ds(c * rows, rows)
            h = _modulated_norm(x_ref[0, sl, :], g_ref[...], sc_ref[0], sh_ref[0])
            h_ref[sl, :] = h
            o_ref[0, sl, :] = (jnp.dot(h, w_ref[...], preferred_element_type=F32) * cs_ref[...]).astype(BF16)

    @pl.when(pl.program_id(2) != 0)
    def _():
        o_ref[0] = (jnp.dot(h_ref[...], w_ref[...], preferred_element_type=F32) * cs_ref[...]).astype(BF16)


QUERY_SCALE = (DA_HEAD_DIM ** -0.5) * math.log2(math.e)


def _inproj(x, ada3, g_mix, w_in, tm=1024, tn=2048, rows=256):
    b, s, _ = x.shape
    assert DA_HEAD_DIM == NA_HEAD_DIM
    col_scale = np.ones((1, IN_W), np.float32)
    col_scale[0, COL_QA:COL_QA + DA_WIDTH] = QUERY_SCALE
    col_scale[0, COL_QN:COL_QN + NA_WIDTH] = QUERY_SCALE
    return pl.pallas_call(
        functools.partial(_inproj_kernel, rows=rows),
        out_shape=jax.ShapeDtypeStruct((b, s, IN_W), BF16),
        grid=(b, s // tm, IN_W // tn),
        in_specs=[
            pl.BlockSpec((1, tm, D_MODEL), lambda bi, i, n: (bi, i, 0)),
            pl.BlockSpec((1, D_MODEL), lambda bi, i, n: (0, 0)),
            pl.BlockSpec((1, 1, D_MODEL), lambda bi, i, n: (bi * 6 + 1, 0, 0)),
            pl.BlockSpec((1, 1, D_MODEL), lambda bi, i, n: (bi * 6 + 0, 0, 0)),
            pl.BlockSpec((D_MODEL, tn), lambda bi, i, n: (0, n)),
            pl.BlockSpec((1, tn), lambda bi, i, n: (0, n)),
        ],
        out_specs=pl.BlockSpec((1, tm, tn), lambda bi, i, n: (bi, i, n)),
        scratch_shapes=[pltpu.VMEM((tm, D_MODEL), BF16)],
        compiler_params=_params(("parallel", "parallel", "arbitrary")),
        name="in_proj",
    )(x, g_mix, ada3, ada3, w_in, jnp.asarray(col_scale))


DA_NCH = 4

def _alibi_lane_constants():
    csum = np.zeros((DA_HEADS,), np.float32)
    aq = np.zeros((DA_HEADS, DA_HEAD_DIM, 1), np.float32)
    ak = np.zeros((DA_HEADS, 1, DA_HEAD_DIM), np.float32)
    for h in range(DA_HEADS):
        slope = 2.0 ** (-ALIBI_MAX_BIAS * (h + 1) / DA_HEADS)
        rem = slope * math.log2(math.e)
        pieces = []
        for _ in range(DA_NCH):
            p = float(np.float32(rem).astype(ml_dtypes.bfloat16).astype(np.float32))
            pieces.append(p)
            rem -= p
        csum[h] = np.float32(sum(pieces))
        for d, weight in enumerate((256.0, 16.0, 1.0)):
            for n, p in enumerate(pieces):
                ak[h, 0, d * DA_NCH + n] = -weight * p
                aq[h, (3 + d) * DA_NCH + n, 0] = p
    return csum, aq, ak


def _da_kernel(csum_ref, lam_ref, aq_ref, ak_ref, q_ref, k_ref, v_ref, g_ref, o_ref,
               acc_ref, qt_ref, ka_ref, vt_ref, corr_ref, s_ref, *, tk):
    h = pl.program_id(1)
    i = pl.program_id(2)
    tq = q_ref.shape[1]
    s_len = k_ref.shape[1]
    n3 = 3 * DA_NCH

    lq = lam_ref[...]
    e1 = jnp.exp(jnp.sum(lq[0:1] * lq[1:2], axis=-1, keepdims=True))
    e2 = jnp.exp(jnp.sum(lq[2:3] * lq[3:4], axis=-1, keepdims=True))
    lam = e1 - e2 + LAMBDA_INIT

    @pl.when(i == 0)
    def _():
        ck = ak_ref[0]
        lane = lax.broadcasted_iota(jnp.int32, (tk, DA_HEAD_DIM), 1)
        row = lax.broadcasted_iota(jnp.int32, (tk, DA_HEAD_DIM), 0)

        def build(r, carry):
            r0 = pl.multiple_of(r * tk, tk)
            pos = row + r * tk
            l2 = lane - n3
            dig = jnp.where(l2 < DA_NCH, (pos >> 8) << 8,
                            jnp.where(l2 < 2 * DA_NCH, ((pos >> 4) & 15) << 4, pos & 15)).astype(F32)
            kpos = jnp.where((lane >= n3) & (lane < 2 * n3), dig, ck).astype(BF16)
            for c in range(2):
                ka_ref[c, pl.ds(r0, tk), 0:DA_HEAD_DIM] = k_ref[0, pl.ds(r0, tk), c * DA_HEAD_DIM:(c + 1) * DA_HEAD_DIM]
                ka_ref[c, pl.ds(r0, tk), DA_HEAD_DIM:] = kpos
            vt_ref[r] = v_ref[0, pl.ds(r0, tk), :].astype(F32).T.astype(BF16)
            return carry

        lax.fori_loop(0, s_len // tk, build, 0)
        gap = (lax.broadcasted_iota(jnp.int32, (tk, tq), 0)
               - lax.broadcasted_iota(jnp.int32, (tk, tq), 1))
        for d in range(2):
            corr_ref[d] = jnp.maximum(gap + d * tk, 0).astype(F32) * (-2.0 * csum_ref[h])

    prow = lax.broadcasted_iota(jnp.int32, (DA_HEAD_DIM, tq), 0)
    pos = lax.broadcasted_iota(jnp.int32, (DA_HEAD_DIM, tq), 1) + i * tq
    dig = jnp.where(prow < DA_NCH, pos >> 8, jnp.where(prow < 2 * DA_NCH, (pos >> 4) & 15, pos & 15)).astype(F32)
    qpos = jnp.where(prow < n3, dig, aq_ref[0])
    for c in range(2):
        qct = q_ref[0, :, c * DA_HEAD_DIM:(c + 1) * DA_HEAD_DIM].astype(F32).T.astype(BF16)
        for sign in range(2):
            qt_ref[sign, c, 0:DA_HEAD_DIM, :] = qct
            qt_ref[sign, c, DA_HEAD_DIM:, :] = (qpos if sign == 0 else -qpos).astype(BF16)
    acc_ref[...] = jnp.zeros_like(acc_ref)

    def scores(j, slot, above, diag=None):
        j0 = pl.multiple_of(j * tk, tk)
        maxes = []
        for c in range(2):
            s = jnp.dot(ka_ref[c, pl.ds(j0, tk), :], qt_ref[above, c], preferred_element_type=F32)
            if diag is not None:
                s = s + corr_ref[diag]
            s_ref[slot, c] = s
            maxes.append(jnp.max(s, axis=0, keepdims=True))
        return tuple(maxes)

    def consume(j, slot, maxes, stats):
        vt = vt_ref[j]
        out = []
        for c in range(2):
            m, l = stats[2 * c], stats[2 * c + 1]
            m_new = jnp.maximum(m, maxes[c])
            alpha = jnp.exp2(m - m_new)
            p = jnp.exp2(s_ref[slot, c] - m_new)
            l_new = alpha * l + jnp.sum(p, axis=0, keepdims=True)
            acc_ref[c] = alpha * acc_ref[c] + jnp.dot(vt, p.astype(BF16), preferred_element_type=F32)
            out += [m_new, l_new]
        return tuple(out)

    nk = s_len // tk

    def off_diagonal(u):
        above = (u >= 2 * i).astype(jnp.int32)
        return u + 2 * above, above

    m_init = jnp.full((1, tq), NEG, F32)
    l_init = jnp.zeros((1, tq), F32)
    mx_d0 = scores(2 * i, 0, 0, diag=0)
    mx_d1 = scores(2 * i + 1, 1, 0, diag=1)
    stats = consume(2 * i, 0, mx_d0, (m_init, l_init, m_init, l_init))

    def pair(p, carry):
        j_prev, mx_prev, stats = carry
        j_a, above = off_diagonal(2 * p)
        mx_a = scores(j_a, 0, above)
        stats = consume(j_prev, 1, mx_prev, stats)
        j_b, above = off_diagonal(2 * p + 1)
        mx_b = scores(j_b, 1, above)
        stats = consume(j_a, 0, mx_a, stats)
        return j_b, mx_b, stats

    trips = jnp.where(i >= 0, (nk - 2) // 2, 0)
    j_last, mx_last, stats = lax.fori_loop(0, trips, pair, (2 * i + 1, mx_d1, stats))
    m0, l0, m1, l1 = consume(j_last, 1, mx_last, stats)

    ot = acc_ref[0] / l0 - lam * (acc_ref[1] / l1)
    ms = jnp.mean(ot * ot, axis=0, keepdims=True)
    y = (ot * lax.rsqrt(ms + EPS)).T * g_ref[...] * (1.0 - LAMBDA_INIT)
    o_ref[0] = y.astype(BF16)


def _diff_attention(proj, lam4, g_subln, tq=1024):
    b, s, _ = proj.shape
    tk = tq // 2
    assert s % tq == 0 and s >= 2 * tq and s <= 16 ** 3
    qb, kb, vb = COL_QA // DA_VDIM, COL_KA // DA_VDIM, COL_VA // DA_VDIM
    csum, aq, ak = _alibi_lane_constants()
    return pl.pallas_call(
        functools.partial(_da_kernel, tk=tk),
        out_shape=jax.ShapeDtypeStruct((b, s, DA_WIDTH), BF16),
        grid=(b, DA_HEADS, s // tq),
        in_specs=[
            pl.BlockSpec(memory_space=pltpu.SMEM),
            pl.BlockSpec((4, DA_HEAD_DIM), lambda bi, h, i: (0, 0)),
            pl.BlockSpec((1, DA_HEAD_DIM, 1), lambda bi, h, i: (h, 0, 0)),
            pl.BlockSpec((1, 1, DA_HEAD_DIM), lambda bi, h, i: (h, 0, 0)),
            pl.BlockSpec((1, tq, DA_VDIM), lambda bi, h, i: (bi, i, qb + h)),
            pl.BlockSpec((1, s, DA_VDIM), lambda bi, h, i: (bi, 0, kb + h)),
            pl.BlockSpec((1, s, DA_VDIM), lambda bi, h, i: (bi, 0, vb + h)),
            pl.BlockSpec((1, DA_VDIM), lambda bi, h, i: (0, 0)),
        ],
        out_specs=pl.BlockSpec((1, tq, DA_VDIM), lambda bi, h, i: (bi, i, h)),
        scratch_shapes=[
            pltpu.VMEM((2, DA_VDIM, tq), F32),
            pltpu.VMEM((2, 2, 2 * DA_HEAD_DIM, tq), BF16),
            pltpu.VMEM((2, s, 2 * DA_HEAD_DIM), BF16),
            pltpu.VMEM((s // tk, DA_VDIM, tk), BF16),
            pltpu.VMEM((2, tk, tq), F32),
            pltpu.VMEM((2, 2, tk, tq), F32),
        ],
        compiler_params=_params(("parallel", "parallel", "arbitrary")),
        name="diff_attn",
    )(jnp.asarray(csum), lam4, jnp.asarray(aq), jnp.asarray(ak), proj, proj, proj, g_subln)


def _na_variant_rows(variant, ri, kr):
    if variant == 0:
        valid, dr = 0 <= kr < NA_ROWS, kr - ri + NA_ROWS - 1
    elif variant == 1:
        valid, dr = ri <= kr < ri + NA_ROWS, kr - ri + NA_ROWS - 1 - NA_GROUP
    else:
        valid, dr = NA_GROUP <= kr < NA_KROWS, kr - ri - 1
    return dr if valid else None


def _na_bias_kernel(rpb_ref, tab_ref, tt_ref):
    h = pl.program_id(0)
    shape = (GRID_W, 2 * GRID_W)
    lane = lax.broadcasted_iota(jnp.int32, shape, 1)
    kcol = lax.broadcasted_iota(jnp.int32, shape, 0)
    col = lane & (GRID_W - 1)
    rel = kcol - col + (NA_COLS - 1)
    cstart = jnp.clip(col - NA_COLS // 2, 0, GRID_W - NA_COLS)
    band = (kcol >= cstart) & (kcol < cstart + NA_COLS)
    neg = jnp.full(shape, NEG, F32)
    for dr in range(NA_RPB_R):
        t = neg
        for dc in range(NA_RPB_C):
            val = rpb_ref[h * (NA_RPB_R * NA_RPB_C) + dr * NA_RPB_C + dc]
            t = jnp.where(rel == dc, val, t)
        tt_ref[dr] = jnp.where(band, t * math.log2(math.e), neg)
    first_half = lane < GRID_W
    for variant in range(3):
        for kr in range(NA_KROWS):
            for pair in range(NA_GROUP // 2):
                da = _na_variant_rows(variant, 2 * pair, kr)
                db = _na_variant_rows(variant, 2 * pair + 1, kr)
                ta = neg if da is None else tt_ref[da]
                tb = neg if db is None else tt_ref[db]
                tab_ref[0, variant, kr * GRID_W:(kr + 1) * GRID_W,
                        pair * 2 * GRID_W:(pair + 1) * 2 * GRID_W] = jnp.where(first_half, ta, tb)


def _na_bias(rpb_flat):
    return pl.pallas_call(
        _na_bias_kernel,
        out_shape=jax.ShapeDtypeStruct((NA_HEADS, 3, NA_K, NA_Q), F32),
        grid=(NA_HEADS,),
        in_specs=[pl.BlockSpec(memory_space=pltpu.SMEM)],
        out_specs=pl.BlockSpec((1, 3, NA_K, NA_Q), lambda h: (h, 0, 0, 0)),
        scratch_shapes=[pltpu.VMEM((NA_RPB_R, GRID_W, 2 * GRID_W), F32)],
        compiler_params=_params(("arbitrary",)),
        name="na_bias",
    )(rpb_flat)


def _na_kernel(q_ref, k_ref, v_ref, tab_ref, o_ref, vt_ref, s_ref, *, n_rows):
    n_groups = n_rows // NA_GROUP
    n_blocks = NA_KROWS // NA_GROUP

    heads = range(q_ref.shape[2] // NA_HEAD_DIM)

    def lanes(hh):
        return slice(hh * NA_HEAD_DIM, (hh + 1) * NA_HEAD_DIM)

    def transpose_values(g):
        g = jnp.minimum(g, n_groups - 1)
        r0 = pl.multiple_of(g * NA_Q, NA_Q)
        for hh in heads:
            vt_ref[hh, g, 0:NA_HEAD_DIM, :] = v_ref[0, pl.ds(r0, NA_Q), lanes(hh)].astype(F32).T.astype(BF16)
            vt_ref[hh, g, NA_HEAD_DIM:, :] = jnp.ones((ONES_ROWS, NA_Q), BF16)

    def window(g):
        return jnp.clip(g - 1, 0, n_groups - n_blocks)

    def scores(g, variant, slot):
        start = pl.multiple_of(window(g) * NA_Q, NA_Q)
        q0 = pl.multiple_of(g * NA_Q, NA_Q)
        maxes = []
        for hh in heads:
            kw = k_ref[0, pl.ds(start, NA_K), lanes(hh)]
            s = lax.dot_general(kw, q_ref[0, pl.ds(q0, NA_Q), lanes(hh)], (((1,), (1,)), ((), ())),
                                preferred_element_type=F32)
            s = s + tab_ref[hh, variant]
            s_ref[hh, slot] = s
            maxes.append(jnp.max(s, axis=0, keepdims=True))
        return tuple(maxes)

    def finish(g, slot, maxes):
        kb = window(g)
        q0 = pl.multiple_of(g * NA_Q, NA_Q)
        for hh in heads:
            pb = jnp.exp2(s_ref[hh, slot] - maxes[hh]).astype(BF16)
            ot = jnp.dot(vt_ref[hh, kb], pb[0:NA_Q], preferred_element_type=F32)
            for t in range(1, n_blocks):
                ot += jnp.dot(vt_ref[hh, kb + t], pb[t * NA_Q:(t + 1) * NA_Q], preferred_element_type=F32)
            l = ot[NA_HEAD_DIM:NA_HEAD_DIM + 1, :]
            o_ref[0, pl.ds(q0, NA_Q), lanes(hh)] = (ot[0:NA_HEAD_DIM, :] / l).T.astype(BF16)

    for g in range(n_blocks):
        transpose_values(g)
    m_first = scores(0, 0, 0)

    def pair(t, carry):
        g_prev, m_prev = carry
        g_a = 1 + 2 * t
        transpose_values(g_a + 2)
        transpose_values(g_a + 3)
        m_a = scores(g_a, 1, 1)
        finish(g_prev, 0, m_prev)
        g_b = g_a + 1
        m_b = scores(g_b, 1, 0)
        finish(g_a, 1, m_a)
        return g_b, m_b

    g_prev, m_prev = lax.fori_loop(0, (n_groups - 2) // 2, pair, (0, m_first))
    m_last = scores(n_groups - 1, 2, 1)
    finish(g_prev, 0, m_prev)
    finish(n_groups - 1, 1, m_last)


def _neighborhood_attention(proj, table):
    b, s, _ = proj.shape
    n_rows = s // GRID_W
    assert n_rows % NA_GROUP == 0 and n_rows >= NA_KROWS and NA_ROWS // 2 == NA_GROUP
    assert NA_KROWS % NA_GROUP == 0 and (n_rows // NA_GROUP) % 2 == 0
    hpb = 4
    width = hpb * NA_HEAD_DIM
    assert NA_HEADS % hpb == 0
    qb, kb, vb = COL_QN // width, COL_KN // width, COL_VN // width
    return pl.pallas_call(
        functools.partial(_na_kernel, n_rows=n_rows),
        out_shape=jax.ShapeDtypeStruct((b, s, NA_WIDTH), BF16),
        grid=(NA_HEADS // hpb, b),
        in_specs=[
            pl.BlockSpec((1, s, width), lambda h, bi: (bi, 0, qb + h)),
            pl.BlockSpec((1, s, width), lambda h, bi: (bi, 0, kb + h)),
            pl.BlockSpec((1, s, width), lambda h, bi: (bi, 0, vb + h)),
            pl.BlockSpec((hpb, 3, NA_K, NA_Q), lambda h, bi: (h, 0, 0, 0), pipeline_mode=pl.Buffered(1)),
        ],
        out_specs=pl.BlockSpec((1, s, width), lambda h, bi: (bi, 0, h)),
        scratch_shapes=[pltpu.VMEM((hpb, n_rows // NA_GROUP, NA_HEAD_DIM + ONES_ROWS, NA_Q), BF16),
                        pltpu.VMEM((hpb, 2, NA_K, NA_Q), F32)],
        compiler_params=_params(("parallel", "parallel")),
        name="nbr_attn",
    )(proj, proj, proj, table)


def _sigmoid(x):
    return 1.0 / (1.0 + jnp.exp(-x))


def _merge_kernel(x_ref, oa_ref, on_ref, ga_ref, gb_ref, wpa_ref, wpb_ref, wout_ref,
                  gt1_ref, gmlp_ref, sc2_ref, sh2_ref, x1_ref, h2_ref, *, parts):
    rows = x_ref.shape[1] // parts
    gain2 = gmlp_ref[...] * (1.0 + sc2_ref[0])
    for r in range(parts):
        sl = pl.ds(r * rows, rows)
        pa = jnp.dot(oa_ref[0, sl, :], wpa_ref[...], preferred_element_type=F32)
        pb = jnp.dot(on_ref[0, sl, :], wpb_ref[...], preferred_element_type=F32)
        merged = (_sigmoid(ga_ref[0, sl, :].astype(F32)) * pa
                  + _sigmoid(gb_ref[0, sl, :].astype(F32)) * pb).astype(BF16)
        out = jnp.dot(merged, wout_ref[...], preferred_element_type=F32)
        x1 = x_ref[0, sl, :] + gt1_ref[0] * out
        x1_ref[0, sl, :] = x1
        ms = jnp.mean(x1 * x1, axis=-1, keepdims=True)
        h2_ref[0, sl, :] = (x1 * lax.rsqrt(ms + EPS) * gain2 + sh2_ref[0]).astype(BF16)


def _merge(x, oa, on, proj, w_pa, w_pb, w_out, ada3, g_mlp, tm=512):
    b, s, _ = x.shape
    ga_b, gb_b = COL_GA // D_MODEL, COL_GB // D_MODEL
    once = pl.Buffered(1)

    def ada_spec(idx):
        return pl.BlockSpec((1, 1, D_MODEL), lambda bi, i: (bi * 6 + idx, 0, 0))

    return pl.pallas_call(
        functools.partial(_merge_kernel, parts=2),
        out_shape=(jax.ShapeDtypeStruct((b, s, D_MODEL), F32),
                   jax.ShapeDtypeStruct((b, s, D_MODEL), BF16)),
        grid=(b, s // tm),
        in_specs=[
            pl.BlockSpec((1, tm, D_MODEL), lambda bi, i: (bi, i, 0)),
            pl.BlockSpec((1, tm, DA_WIDTH), lambda bi, i: (bi, i, 0)),
            pl.BlockSpec((1, tm, NA_WIDTH), lambda bi, i: (bi, i, 0)),
            pl.BlockSpec((1, tm, D_MODEL), lambda bi, i: (bi, i, ga_b)),
            pl.BlockSpec((1, tm, D_MODEL), lambda bi, i: (bi, i, gb_b)),
            pl.BlockSpec((DA_WIDTH, D_MODEL), lambda bi, i: (0, 0), pipeline_mode=once),
            pl.BlockSpec((NA_WIDTH, D_MODEL), lambda bi, i: (0, 0), pipeline_mode=once),
            pl.BlockSpec((D_MODEL, D_MODEL), lambda bi, i: (0, 0), pipeline_mode=once),
            ada_spec(2),
            pl.BlockSpec((1, D_MODEL), lambda bi, i: (0, 0)),
            ada_spec(4),
            ada_spec(3),
        ],
        out_specs=(pl.BlockSpec((1, tm, D_MODEL), lambda bi, i: (bi, i, 0)),
                   pl.BlockSpec((1, tm, D_MODEL), lambda bi, i: (bi, i, 0))),
        compiler_params=_params(("parallel", "parallel")),
        name="merge_out_proj",
    )(x, oa, on, proj, proj, w_pa, w_pb, w_out, ada3, g_mlp, ada3, ada3)


def _mlp_kernel(h2_ref, w1_ref, w2_ref, x1_ref, gt2_ref, gf_ref, y_ref, *, rows):
    f = pl.program_id(2)
    last = pl.num_programs(2) - 1

    def contribution(sl):
        u = jnp.dot(h2_ref[0, sl, :], w1_ref[...], preferred_element_type=F32)
        u = jnp.square(jnp.maximum(u, 0.0)).astype(BF16)
        return jnp.dot(u, w2_ref[...], preferred_element_type=F32)

    @pl.when(f == 0)
    def _():
        y_ref[0] = contribution(slice(None))

    @pl.when((f != 0) & (f != last))
    def _():
        y_ref[0] += contribution(slice(None))

    @pl.when(f == last)
    def _():
        for c in range(y_ref.shape[1] // rows):
            sl = pl.ds(c * rows, rows)
            x2 = x1_ref[0, sl, :] + gt2_ref[0] * (y_ref[0, sl, :] + contribution(sl))
            ms = jnp.mean(x2 * x2, axis=-1, keepdims=True)
            y_ref[0, sl, :] = x2 * lax.rsqrt(ms + EPS) * gf_ref[...]


def _mlp(h2, x1, w1, w2, ada3, g_final, tm=512, tf=1024):
    b, s, _ = x1.shape
    return pl.pallas_call(
        functools.partial(_mlp_kernel, rows=tm // 2),
        out_shape=jax.ShapeDtypeStruct((b, s, D_MODEL), F32),
        grid=(b, s // tm, D_FF // tf),
        in_specs=[
            pl.BlockSpec((1, tm, D_MODEL), lambda bi, i, f: (bi, i, 0)),
            pl.BlockSpec((D_MODEL, tf), lambda bi, i, f: (0, f)),
            pl.BlockSpec((tf, D_MODEL), lambda bi, i, f: (f, 0)),
            pl.BlockSpec((1, tm, D_MODEL), lambda bi, i, f: (bi, i, 0)),
            pl.BlockSpec((1, 1, D_MODEL), lambda bi, i, f: (bi * 6 + 5, 0, 0)),
            pl.BlockSpec((1, D_MODEL), lambda bi, i, f: (0, 0)),
        ],
        out_specs=pl.BlockSpec((1, tm, D_MODEL), lambda bi, i, f: (bi, i, 0)),
        compiler_params=_params(("parallel", "parallel", "arbitrary")),
        name="mlp_final_norm",
    )(h2, w1, w2, x1, ada3, g_final)


def _trunk(x, ada, w, table, lam4):
    b = x.shape[0]
    ada3 = ada.reshape(b * 6, 1, D_MODEL)
    proj = _inproj(x, ada3, w["g_mix"], w["w_in"])
    oa = _diff_attention(proj, lam4, w["g_subln"])
    on = _neighborhood_attention(proj, table)
    x1, h2 = _merge(x, oa, on, proj, w["w_pa"], w["w_pb"], w["w_out"], ada3, w["g_mlp"])
    return _mlp(h2, x1, w["w1"], w["w2"], ada3, w["g_final"])


def kernel(x_prompt, x_sample, c_prompt, c_sample, w_ada, b_ada, g_mix, w_in, lam_q1, lam_k1, lam_q2, lam_k2, g_subln, rpb, w_pa, w_pb, w_out, g_mlp, w1, w2, g_final):
    w = {
        "g_mix": g_mix[0].reshape(1, D_MODEL),
        "w_in": w_in[0].astype(BF16),
        "g_subln": g_subln[0].reshape(1, DA_VDIM),
        "w_pa": w_pa[0].astype(BF16),
        "w_pb": w_pb[0].astype(BF16),
        "w_out": w_out[0].astype(BF16),
        "g_mlp": g_mlp[0].reshape(1, D_MODEL),
        "w1": w1[0].astype(BF16),
        "w2": w2[0].astype(BF16),
        "g_final": g_final.reshape(1, D_MODEL),
    }
    nb_p = c_prompt.shape[0]
    ada = _ada(jnp.concatenate([c_prompt, c_sample], axis=0), w_ada[0], b_ada[0])
    table = _na_bias(rpb[0].reshape(-1))
    lam4 = jnp.stack([lam_q1[0], lam_k1[0], lam_q2[0], lam_k2[0]], axis=0)
    y_prompt = _trunk(x_prompt, ada[:nb_p], w, table, lam4)
    y_sample = _trunk(x_sample, ada[nb_p:], w, table, lam4)
    return (y_prompt, y_sample)
```

```python
import functools
import math

import jax
import jax.numpy as jnp
import ml_dtypes
import numpy as np
from jax import lax
from jax.experimental import pallas as pl
from jax.experimental.pallas import tpu as pltpu

F32 = jnp.float32
BF16 = jnp.bfloat16

D_MODEL = 2048
DA_HEADS = 4
DA_HEAD_DIM = 128
DA_VDIM = 256
DA_WIDTH = DA_HEADS * DA_VDIM
NA_HEADS = 8
NA_HEAD_DIM = 128
NA_WIDTH = NA_HEADS * NA_HEAD_DIM
GRID_W = 64
NA_ROWS = 8
NA_COLS = 16
D_FF = 4 * D_MODEL
ALIBI_MAX_BIAS = 8.0
EPS = 1e-6
IN_W = 3 * DA_WIDTH + 3 * NA_WIDTH + 2 * D_MODEL
LAMBDA_INIT = 0.8 - 0.6 * math.exp(-0.3 * 0)

COL_QA, COL_KA, COL_VA = 0, DA_WIDTH, 2 * DA_WIDTH
COL_QN = 3 * DA_WIDTH
COL_KN = COL_QN + NA_WIDTH
COL_VN = COL_KN + NA_WIDTH
COL_GA = COL_VN + NA_WIDTH
COL_GB = COL_GA + D_MODEL

NA_GROUP = 4
NA_KROWS = NA_GROUP + NA_ROWS
NA_Q = NA_GROUP * GRID_W
NA_K = NA_KROWS * GRID_W
NA_RPB_R = 2 * NA_ROWS - 1
NA_RPB_C = 2 * NA_COLS - 1
NEG = -1e30
ONES_ROWS = 16

VMEM_LIMIT = 56 * 1024 * 1024


def _params(sem):
    return pltpu.CompilerParams(dimension_semantics=sem, vmem_limit_bytes=VMEM_LIMIT)


def _ada_kernel(c_ref, w_ref, b_ref, o_ref):
    c = c_ref[...].astype(BF16)
    w = w_ref[...].astype(BF16)
    o_ref[...] = jnp.dot(c, w, preferred_element_type=F32) + b_ref[...]


def _ada(c_all, w_ada, b_ada, tn=1024):
    nb = c_all.shape[0]
    n = w_ada.shape[1]
    return pl.pallas_call(
        _ada_kernel,
        out_shape=jax.ShapeDtypeStruct((nb, n), F32),
        grid=(n // tn,),
        in_specs=[
            pl.BlockSpec((nb, D_MODEL), lambda j: (0, 0)),
            pl.BlockSpec((D_MODEL, tn), lambda j: (0, j)),
            pl.BlockSpec((1, tn), lambda j: (0, j)),
        ],
        out_specs=pl.BlockSpec((nb, tn), lambda j: (0, j)),
        compiler_params=_params(("parallel",)),
        name="ada_proj",
    )(c_all, w_ada, b_ada.reshape(1, n))


def _modulated_norm(x, g, sc, sh):
    ms = jnp.mean(x * x, axis=-1, keepdims=True)
    return ((x * lax.rsqrt(ms + EPS) * g) * (1.0 + sc) + sh).astype(BF16)


def _inproj_kernel(x_ref, g_ref, sc_ref, sh_ref, w_ref, cs_ref, o_ref, h_ref, *, rows):
    @pl.when(pl.program_id(2) == 0)
    def _():
        for c in range(x_ref.shape[1] // rows):
            sl = pl.ds(c * rows, rows)
            h = _modulated_norm(x_ref[0, sl, :], g_ref[...], sc_ref[0], sh_ref[0])
            h_ref[sl, :] = h
            o_ref[0, sl, :] = (jnp.dot(h, w_ref[...], preferred_element_type=F32) * cs_ref[...]).astype(BF16)

    @pl.when(pl.program_id(2) != 0)
    def _():
        o_ref[0] = (jnp.dot(h_ref[...], w_ref[...], preferred_element_type=F32) * cs_ref[...]).astype(BF16)


QUERY_SCALE = (DA_HEAD_DIM ** -0.5) * math.log2(math.e)


def _inproj(x, ada3, g_mix, w_in, tm=1024, tn=2048, rows=256):
    b, s, _ = x.shape
    assert DA_HEAD_DIM == NA_HEAD_DIM
    col_scale = np.ones((1, IN_W), np.float32)
    col_scale[0, COL_QA:COL_QA + DA_WIDTH] = QUERY_SCALE
    col_scale[0, COL_QN:COL_QN + NA_WIDTH] = QUERY_SCALE
    return pl.pallas_call(
        functools.partial(_inproj_kernel, rows=rows),
        out_shape=jax.ShapeDtypeStruct((b, s, IN_W), BF16),
        grid=(b, s // tm, IN_W // tn),
        in_specs=[
            pl.BlockSpec((1, tm, D_MODEL), lambda bi, i, n: (bi, i, 0)),
            pl.BlockSpec((1, D_MODEL), lambda bi, i, n: (0, 0)),
            pl.BlockSpec((1, 1, D_MODEL), lambda bi, i, n: (bi * 6 + 1, 0, 0)),
            pl.BlockSpec((1, 1, D_MODEL), lambda bi, i, n: (bi * 6 + 0, 0, 0)),
            pl.BlockSpec((D_MODEL, tn), lambda bi, i, n: (0, n)),
            pl.BlockSpec((1, tn), lambda bi, i, n: (0, n)),
        ],
        out_specs=pl.BlockSpec((1, tm, tn), lambda bi, i, n: (bi, i, n)),
        scratch_shapes=[pltpu.VMEM((tm, D_MODEL), BF16)],
        compiler_params=_params(("parallel", "parallel", "arbitrary")),
        name="in_proj",
    )(x, g_mix, ada3, ada3, w_in, jnp.asarray(col_scale))


DA_NCH = 4

def _alibi_lane_constants():
    csum = np.zeros((DA_HEADS,), np.float32)
    aq = np.zeros((DA_HEADS, DA_HEAD_DIM, 1), np.float32)
    ak = np.zeros((DA_HEADS, 1, DA_HEAD_DIM), np.float32)
    for h in range(DA_HEADS):
        slope = 2.0 ** (-ALIBI_MAX_BIAS * (h + 1) / DA_HEADS)
        rem = slope * math.log2(math.e)
        pieces = []
        for _ in range(DA_NCH):
            p = float(np.float32(rem).astype(ml_dtypes.bfloat16).astype(np.float32))
            pieces.append(p)
            rem -= p
        csum[h] = np.float32(sum(pieces))
        for d, weight in enumerate((256.0, 16.0, 1.0)):
            for n, p in enumerate(pieces):
                ak[h, 0, d * DA_NCH + n] = -weight * p
                aq[h, (3 + d) * DA_NCH + n, 0] = p
    return csum, aq, ak


def _da_kernel(csum_ref, lam_ref, aq_ref, ak_ref, q_ref, k_ref, v_ref, g_ref, o_ref,
               acc_ref, qt_ref, ka_ref, corr_ref, s_ref, *, tk):
    h = pl.program_id(1)
    i = pl.program_id(2)
    tq = q_ref.shape[1]
    s_len = k_ref.shape[1]
    n3 = 3 * DA_NCH

    lq = lam_ref[...]
    e1 = jnp.exp(jnp.sum(lq[0:1] * lq[1:2], axis=-1, keepdims=True))
    e2 = jnp.exp(jnp.sum(lq[2:3] * lq[3:4], axis=-1, keepdims=True))
    lam = e1 - e2 + LAMBDA_INIT

    @pl.when(i == 0)
    def _():
        ck = ak_ref[0]
        lane = lax.broadcasted_iota(jnp.int32, (tk, DA_HEAD_DIM), 1)
        row = lax.broadcasted_iota(jnp.int32, (tk, DA_HEAD_DIM), 0)

        def build(r, carry):
            r0 = pl.multiple_of(r * tk, tk)
            pos = row + r * tk
            l2 = lane - n3
            dig = jnp.where(l2 < DA_NCH, (pos >> 8) << 8,
                            jnp.where(l2 < 2 * DA_NCH, ((pos >> 4) & 15) << 4, pos & 15)).astype(F32)
            kpos = jnp.where((lane >= n3) & (lane < 2 * n3), dig, ck).astype(BF16)
            for c in range(2):
                ka_ref[c, pl.ds(r0, tk), 0:DA_HEAD_DIM] = k_ref[0, pl.ds(r0, tk), c * DA_HEAD_DIM:(c + 1) * DA_HEAD_DIM]
                ka_ref[c, pl.ds(r0, tk), DA_HEAD_DIM:] = kpos
            return carry

        lax.fori_loop(0, s_len // tk, build, 0)
        gap = (lax.broadcasted_iota(jnp.int32, (tk, tq), 0)
               - lax.broadcasted_iota(jnp.int32, (tk, tq), 1))
        for d in range(2):
            corr_ref[d] = jnp.maximum(gap + d * tk, 0).astype(F32) * (-2.0 * csum_ref[h])

    prow = lax.broadcasted_iota(jnp.int32, (DA_HEAD_DIM, tq), 0)
    pos = lax.broadcasted_iota(jnp.int32, (DA_HEAD_DIM, tq), 1) + i * tq
    dig = jnp.where(prow < DA_NCH, pos >> 8, jnp.where(prow < 2 * DA_NCH, (pos >> 4) & 15, pos & 15)).astype(F32)
    qpos = jnp.where(prow < n3, dig, aq_ref[0])
    for c in range(2):
        qct = q_ref[0, :, c * DA_HEAD_DIM:(c + 1) * DA_HEAD_DIM].astype(F32).T.astype(BF16)
        for sign in range(2):
            qt_ref[sign, c, 0:DA_HEAD_DIM, :] = qct
            qt_ref[sign, c, DA_HEAD_DIM:, :] = (qpos if sign == 0 else -qpos).astype(BF16)
    acc_ref[...] = jnp.zeros_like(acc_ref)

    def scores(j, slot, above, diag=None):
        j0 = pl.multiple_of(j * tk, tk)
        maxes = []
        for c in range(2):
            s = jnp.dot(ka_ref[c, pl.ds(j0, tk), :], qt_ref[above, c], preferred_element_type=F32)
            if diag is not None:
                s = s + corr_ref[diag]
            s_ref[slot, c] = s
            maxes.append(jnp.max(s, axis=0, keepdims=True))
        return tuple(maxes)

    def consume(j, slot, maxes, stats):
        vc = v_ref[0, pl.ds(pl.multiple_of(j * tk, tk), tk), :]
        out = []
        for c in range(2):
            m, l = stats[2 * c], stats[2 * c + 1]
            m_new = jnp.maximum(m, maxes[c])
            alpha = jnp.exp2(m - m_new)
            p = jnp.exp2(s_ref[slot, c] - m_new)
            l_new = alpha * l + jnp.sum(p, axis=0, keepdims=True)
            acc_ref[c] = alpha * acc_ref[c] + lax.dot_general(
                vc, p.astype(BF16), (((0,), (0,)), ((), ())), preferred_element_type=F32)
            out += [m_new, l_new]
        return tuple(out)

    nk = s_len // tk

    def off_diagonal(u):
        above = (u >= 2 * i).astype(jnp.int32)
        return u + 2 * above, above

    m_init = jnp.full((1, tq), NEG, F32)
    l_init = jnp.zeros((1, tq), F32)
    mx_d0 = scores(2 * i, 0, 0, diag=0)
    mx_d1 = scores(2 * i + 1, 1, 0, diag=1)
    stats = consume(2 * i, 0, mx_d0, (m_init, l_init, m_init, l_init))

    def pair(p, carry):
        j_prev, mx_prev, stats = carry
        j_a, above = off_diagonal(2 * p)
        mx_a = scores(j_a, 0, above)
        stats = consume(j_prev, 1, mx_prev, stats)
        j_b, above = off_diagonal(2 * p + 1)
        mx_b = scores(j_b, 1, above)
        stats = consume(j_a, 0, mx_a, stats)
        return j_b, mx_b, stats

    trips = jnp.where(i >= 0, (nk - 2) // 2, 0)
    j_last, mx_last, stats = lax.fori_loop(0, trips, pair, (2 * i + 1, mx_d1, stats))
    m0, l0, m1, l1 = consume(j_last, 1, mx_last, stats)

    ot = acc_ref[0] / l0 - lam * (acc_ref[1] / l1)
    ms = jnp.mean(ot * ot, axis=0, keepdims=True)
    y = (ot * lax.rsqrt(ms + EPS)).T * g_ref[...] * (1.0 - LAMBDA_INIT)
    o_ref[0] = y.astype(BF16)


def _diff_attention(proj, lam4, g_subln, tq=1024):
    b, s, _ = proj.shape
    tk = tq // 2
    assert s % tq == 0 and s >= 2 * tq and s <= 16 ** 3
    qb, kb, vb = COL_QA // DA_VDIM, COL_KA // DA_VDIM, COL_VA // DA_VDIM
    csum, aq, ak = _alibi_lane_constants()
    return pl.pallas_call(
        functools.partial(_da_kernel, tk=tk),
        out_shape=jax.ShapeDtypeStruct((b, s, DA_WIDTH), BF16),
        grid=(b, DA_HEADS, s // tq),
        in_specs=[
            pl.BlockSpec(memory_space=pltpu.SMEM),
            pl.BlockSpec((4, DA_HEAD_DIM), lambda bi, h, i: (0, 0)),
            pl.BlockSpec((1, DA_HEAD_DIM, 1), lambda bi, h, i: (h, 0, 0)),
            pl.BlockSpec((1, 1, DA_HEAD_DIM), lambda bi, h, i: (h, 0, 0)),
            pl.BlockSpec((1, tq, DA_VDIM), lambda bi, h, i: (bi, i, qb + h)),
            pl.BlockSpec((1, s, DA_VDIM), lambda bi, h, i: (bi, 0, kb + h)),
            pl.BlockSpec((1, s, DA_VDIM), lambda bi, h, i: (bi, 0, vb + h)),
            pl.BlockSpec((1, DA_VDIM), lambda bi, h, i: (0, 0)),
        ],
        out_specs=pl.BlockSpec((1, tq, DA_VDIM), lambda bi, h, i: (bi, i, h)),
        scratch_shapes=[
            pltpu.VMEM((2, DA_VDIM, tq), F32),
            pltpu.VMEM((2, 2, 2 * DA_HEAD_DIM, tq), BF16),
            pltpu.VMEM((2, s, 2 * DA_HEAD_DIM), BF16),
            pltpu.VMEM((2, tk, tq), F32),
            pltpu.VMEM((2, 2, tk, tq), F32),
        ],
        compiler_params=_params(("parallel", "parallel", "arbitrary")),
        name="diff_attn",
    )(jnp.asarray(csum), lam4, jnp.asarray(aq), jnp.asarray(ak), proj, proj, proj, g_subln)


def _na_variant_rows(variant, ri, kr):
    if variant == 0:
        valid, dr = 0 <= kr < NA_ROWS, kr - ri + NA_ROWS - 1
    elif variant == 1:
        valid, dr = ri <= kr < ri + NA_ROWS, kr - ri + NA_ROWS - 1 - NA_GROUP
    else:
        valid, dr = NA_GROUP <= kr < NA_KROWS, kr - ri - 1
    return dr if valid else None


def _na_bias_kernel(rpb_ref, tab_ref, tt_ref):
    h = pl.program_id(0)
    shape = (GRID_W, 2 * GRID_W)
    lane = lax.broadcasted_iota(jnp.int32, shape, 1)
    kcol = lax.broadcasted_iota(jnp.int32, shape, 0)
    col = lane & (GRID_W - 1)
    rel = kcol - col + (NA_COLS - 1)
    cstart = jnp.clip(col - NA_COLS // 2, 0, GRID_W - NA_COLS)
    band = (kcol >= cstart) & (kcol < cstart + NA_COLS)
    neg = jnp.full(shape, NEG, F32)
    for dr in range(NA_RPB_R):
        t = neg
        for dc in range(NA_RPB_C):
            val = rpb_ref[h * (NA_RPB_R * NA_RPB_C) + dr * NA_RPB_C + dc]
            t = jnp.where(rel == dc, val, t)
        tt_ref[dr] = jnp.where(band, t * math.log2(math.e), neg)
    first_half = lane < GRID_W
    for variant in range(3):
        for kr in range(NA_KROWS):
            for pair in range(NA_GROUP // 2):
                da = _na_variant_rows(variant, 2 * pair, kr)
                db = _na_variant_rows(variant, 2 * pair + 1, kr)
                ta = neg if da is None else tt_ref[da]
                tb = neg if db is None else tt_ref[db]
                tab_ref[0, variant, kr * GRID_W:(kr + 1) * GRID_W,
                        pair * 2 * GRID_W:(pair + 1) * 2 * GRID_W] = jnp.where(first_half, ta, tb)


def _na_bias(rpb_flat):
    return pl.pallas_call(
        _na_bias_kernel,
        out_shape=jax.ShapeDtypeStruct((NA_HEADS, 3, NA_K, NA_Q), F32),
        grid=(NA_HEADS,),
        in_specs=[pl.BlockSpec(memory_space=pltpu.SMEM)],
        out_specs=pl.BlockSpec((1, 3, NA_K, NA_Q), lambda h: (h, 0, 0, 0)),
        scratch_shapes=[pltpu.VMEM((NA_RPB_R, GRID_W, 2 * GRID_W), F32)],
        compiler_params=_params(("arbitrary",)),
        name="na_bias",
    )(rpb_flat)


def _na_kernel(q_ref, k_ref, v_ref, tab_ref, o_ref, vt_ref, s_ref, *, n_rows):
    n_groups = n_rows // NA_GROUP
    n_blocks = NA_KROWS // NA_GROUP

    heads = range(q_ref.shape[2] // NA_HEAD_DIM)

    def lanes(hh):
        return slice(hh * NA_HEAD_DIM, (hh + 1) * NA_HEAD_DIM)

    def transpose_values(g):
        g = jnp.minimum(g, n_groups - 1)
        r0 = pl.multiple_of(g * NA_Q, NA_Q)
        for hh in heads:
            vt_ref[hh, g, 0:NA_HEAD_DIM, :] = v_ref[0, pl.ds(r0, NA_Q), lanes(hh)].astype(F32).T.astype(BF16)
            vt_ref[hh, g, NA_HEAD_DIM:, :] = jnp.ones((ONES_ROWS, NA_Q), BF16)

    def window(g):
        return jnp.clip(g - 1, 0, n_groups - n_blocks)

    def scores(g, variant, slot):
        start = pl.multiple_of(window(g) * NA_Q, NA_Q)
        q0 = pl.multiple_of(g * NA_Q, NA_Q)
        maxes = []
        for hh in heads:
            kw = k_ref[0, pl.ds(start, NA_K), lanes(hh)]
            s = lax.dot_general(kw, q_ref[0, pl.ds(q0, NA_Q), lanes(hh)], (((1,), (1,)), ((), ())),
                                preferred_element_type=F32)
            s = s + tab_ref[hh, variant]
            s_ref[hh, slot] = s
            maxes.append(jnp.max(s, axis=0, keepdims=True))
        return tuple(maxes)

    def finish(g, slot, maxes):
        kb = window(g)
        q0 = pl.multiple_of(g * NA_Q, NA_Q)
        for hh in heads:
            pb = jnp.exp2(s_ref[hh, slot] - maxes[hh]).astype(BF16)
            ot = jnp.dot(vt_ref[hh, kb], pb[0:NA_Q], preferred_element_type=F32)
            for t in range(1, n_blocks):
                ot += jnp.dot(vt_ref[hh, kb + t], pb[t * NA_Q:(t + 1) * NA_Q], preferred_element_type=F32)
            l = ot[NA_HEAD_DIM:NA_HEAD_DIM + 1, :]
            o_ref[0, pl.ds(q0, NA_Q), lanes(hh)] = (ot[0:NA_HEAD_DIM, :] / l).T.astype(BF16)

    for g in range(n_blocks):
        transpose_values(g)
    m_first = scores(0, 0, 0)

    def pair(t, carry):
        g_prev, m_prev = carry
        g_a = 1 + 2 * t
        transpose_values(g_a + 2)
        transpose_values(g_a + 3)
        m_a = scores(g_a, 1, 1)
        finish(g_prev, 0, m_prev)
        g_b = g_a + 1
        m_b = scores(g_b, 1, 0)
        finish(g_a, 1, m_a)
        return g_b, m_b

    g_prev, m_prev = lax.fori_loop(0, (n_groups - 2) // 2, pair, (0, m_first))
    m_last = scores(n_groups - 1, 2, 1)
    finish(g_prev, 0, m_prev)
    finish(n_groups - 1, 1, m_last)


def _neighborhood_attention(proj, table):
    b, s, _ = proj.shape
    n_rows = s // GRID_W
    assert n_rows % NA_GROUP == 0 and n_rows >= NA_KROWS and NA_ROWS // 2 == NA_GROUP
    assert NA_KROWS % NA_GROUP == 0 and (n_rows // NA_GROUP) % 2 == 0
    hpb = 4
    width = hpb * NA_HEAD_DIM
    assert NA_HEADS % hpb == 0
    qb, kb, vb = COL_QN // width, COL_KN // width, COL_VN // width
    return pl.pallas_call(
        functools.partial(_na_kernel, n_rows=n_rows),
        out_shape=jax.ShapeDtypeStruct((b, s, NA_WIDTH), BF16),
        grid=(NA_HEADS // hpb, b),
        in_specs=[
            pl.BlockSpec((1, s, width), lambda h, bi: (bi, 0, qb + h)),
            pl.BlockSpec((1, s, width), lambda h, bi: (bi, 0, kb + h)),
            pl.BlockSpec((1, s, width), lambda h, bi: (bi, 0, vb + h)),
            pl.BlockSpec((hpb, 3, NA_K, NA_Q), lambda h, bi: (h, 0, 0, 0), pipeline_mode=pl.Buffered(1)),
        ],
        out_specs=pl.BlockSpec((1, s, width), lambda h, bi: (bi, 0, h)),
        scratch_shapes=[pltpu.VMEM((hpb, n_rows // NA_GROUP, NA_HEAD_DIM + ONES_ROWS, NA_Q), BF16),
                        pltpu.VMEM((hpb, 2, NA_K, NA_Q), F32)],
        compiler_params=_params(("parallel", "parallel")),
        name="nbr_attn",
    )(proj, proj, proj, table)


def _sigmoid(x):
    return 1.0 / (1.0 + jnp.exp(-x))


def _merge_kernel(x_ref, oa_ref, on_ref, ga_ref, gb_ref, wpa_ref, wpb_ref, wout_ref,
                  gt1_ref, gmlp_ref, sc2_ref, sh2_ref, x1_ref, h2_ref, *, parts):
    rows = x_ref.shape[1] // parts
    gain2 = gmlp_ref[...] * (1.0 + sc2_ref[0])
    for r in range(parts):
        sl = pl.ds(r * rows, rows)
        pa = jnp.dot(oa_ref[0, sl, :], wpa_ref[...], preferred_element_type=F32)
        pb = jnp.dot(on_ref[0, sl, :], wpb_ref[...], preferred_element_type=F32)
        merged = (_sigmoid(ga_ref[0, sl, :].astype(F32)) * pa
                  + _sigmoid(gb_ref[0, sl, :].astype(F32)) * pb).astype(BF16)
        out = jnp.dot(merged, wout_ref[...], preferred_element_type=F32)
        x1 = x_ref[0, sl, :] + gt1_ref[0] * out
        x1_ref[0, sl, :] = x1
        ms = jnp.mean(x1 * x1, axis=-1, keepdims=True)
        h2_ref[0, sl, :] = (x1 * lax.rsqrt(ms + EPS) * gain2 + sh2_ref[0]).astype(BF16)


def _merge(x, oa, on, proj, w_pa, w_pb, w_out, ada3, g_mlp, tm=512):
    b, s, _ = x.shape
    ga_b, gb_b = COL_GA // D_MODEL, COL_GB // D_MODEL
    once = pl.Buffered(1)

    def ada_spec(idx):
        return pl.BlockSpec((1, 1, D_MODEL), lambda bi, i: (bi * 6 + idx, 0, 0))

    return pl.pallas_call(
        functools.partial(_merge_kernel, parts=2),
        out_shape=(jax.ShapeDtypeStruct((b, s, D_MODEL), F32),
                   jax.ShapeDtypeStruct((b, s, D_MODEL), BF16)),
        grid=(b, s // tm),
        in_specs=[
            pl.BlockSpec((1, tm, D_MODEL), lambda bi, i: (bi, i, 0)),
            pl.BlockSpec((1, tm, DA_WIDTH), lambda bi, i: (bi, i, 0)),
            pl.BlockSpec((1, tm, NA_WIDTH), lambda bi, i: (bi, i, 0)),
            pl.BlockSpec((1, tm, D_MODEL), lambda bi, i: (bi, i, ga_b)),
            pl.BlockSpec((1, tm, D_MODEL), lambda bi, i: (bi, i, gb_b)),
            pl.BlockSpec((DA_WIDTH, D_MODEL), lambda bi, i: (0, 0), pipeline_mode=once),
            pl.BlockSpec((NA_WIDTH, D_MODEL), lambda bi, i: (0, 0), pipeline_mode=once),
            pl.BlockSpec((D_MODEL, D_MODEL), lambda bi, i: (0, 0), pipeline_mode=once),
            ada_spec(2),
            pl.BlockSpec((1, D_MODEL), lambda bi, i: (0, 0)),
            ada_spec(4),
            ada_spec(3),
        ],
        out_specs=(pl.BlockSpec((1, tm, D_MODEL), lambda bi, i: (bi, i, 0)),
                   pl.BlockSpec((1, tm, D_MODEL), lambda bi, i: (bi, i, 0))),
        compiler_params=_params(("parallel", "parallel")),
        name="merge_out_proj",
    )(x, oa, on, proj, proj, w_pa, w_pb, w_out, ada3, g_mlp, ada3, ada3)


def _mlp_kernel(h2_ref, w1_ref, w2_ref, x1_ref, gt2_ref, gf_ref, y_ref, *, rows):
    f = pl.program_id(2)
    last = pl.num_programs(2) - 1

    def contribution(sl):
        u = jnp.dot(h2_ref[0, sl, :], w1_ref[...], preferred_element_type=F32)
        u = jnp.square(jnp.maximum(u, 0.0)).astype(BF16)
        return jnp.dot(u, w2_ref[...], preferred_element_type=F32)

    @pl.when(f == 0)
    def _():
        y_ref[0] = contribution(slice(None))

    @pl.when((f != 0) & (f != last))
    def _():
        y_ref[0] += contribution(slice(None))

    @pl.when(f == last)
    def _():
        for c in range(y_ref.shape[1] // rows):
            sl = pl.ds(c * rows, rows)
            x2 = x1_ref[0, sl, :] + gt2_ref[0] * (y_ref[0, sl, :] + contribution(sl))
            ms = jnp.mean(x2 * x2, axis=-1, keepdims=True)
            y_ref[0, sl, :] = x2 * lax.rsqrt(ms + EPS) * gf_ref[...]


def _mlp(h2, x1, w1, w2, ada3, g_final, tm=512, tf=1024):
    b, s, _ = x1.shape
    return pl.pallas_call(
        functools.partial(_mlp_kernel, rows=tm // 2),
        out_shape=jax.ShapeDtypeStruct((b, s, D_MODEL), F32),
        grid=(b, s // tm, D_FF // tf),
        in_specs=[
            pl.BlockSpec((1, tm, D_MODEL), lambda bi, i, f: (bi, i, 0)),
            pl.BlockSpec((D_MODEL, tf), lambda bi, i, f: (0, f)),
            pl.BlockSpec((tf, D_MODEL), lambda bi, i, f: (f, 0)),
            pl.BlockSpec((1, tm, D_MODEL), lambda bi, i, f: (bi, i, 0)),
            pl.BlockSpec((1, 1, D_MODEL), lambda bi, i, f: (bi * 6 + 5, 0, 0)),
            pl.BlockSpec((1, D_MODEL), lambda bi, i, f: (0, 0)),
        ],
        out_specs=pl.BlockSpec((1, tm, D_MODEL), lambda bi, i, f: (bi, i, 0)),
        compiler_params=_params(("parallel", "parallel", "arbitrary")),
        name="mlp_final_norm",
    )(h2, w1, w2, x1, ada3, g_final)


def _trunk(x, ada, w, table, lam4):
    b = x.shape[0]
    ada3 = ada.reshape(b * 6, 1, D_MODEL)
    proj = _inproj(x, ada3, w["g_mix"], w["w_in"])
    oa = _diff_attention(proj, lam4, w["g_subln"])
    on = _neighborhood_attention(proj, table)
    x1, h2 = _merge(x, oa, on, proj, w["w_pa"], w["w_pb"], w["w_out"], ada3, w["g_mlp"])
    return _mlp(h2, x1, w["w1"], w["w2"], ada3, w["g_final"])


def kernel(x_prompt, x_sample, c_prompt, c_sample, w_ada, b_ada, g_mix, w_in, lam_q1, lam_k1, lam_q2, lam_k2, g_subln, rpb, w_pa, w_pb, w_out, g_mlp, w1, w2, g_final):
    w = {
        "g_mix": g_mix[0].reshape(1, D_MODEL),
        "w_in": w_in[0].astype(BF16),
        "g_subln": g_subln[0].reshape(1, DA_VDIM),
        "w_pa": w_pa[0].astype(BF16),
        "w_pb": w_pb[0].astype(BF16),
        "w_out": w_out[0].astype(BF16),
        "g_mlp": g_mlp[0].reshape(1, D_MODEL),
        "w1": w1[0].astype(BF16),
        "w2": w2[0].astype(BF16),
        "g_final": g_final.reshape(1, D_MODEL),
    }
    nb_p = c_prompt.shape[0]
    ada = _ada(jnp.concatenate([c_prompt, c_sample], axis=0), w_ada[0], b_ada[0])
    table = _na_bias(rpb[0].reshape(-1))
    lam4 = jnp.stack([lam_q1[0], lam_k1[0], lam_q2[0], lam_k2[0]], axis=0)
    y_prompt = _trunk(x_prompt, ada[:nb_p], w, table, lam4)
    y_sample = _trunk(x_sample, ada[nb_p:], w, table, lam4)
    return (y_prompt, y_sample)
```

```python
import functools
import math

import jax
import jax.numpy as jnp
import ml_dtypes
import numpy as np
from jax import lax
from jax.experimental import pallas as pl
from jax.experimental.pallas import tpu as pltpu

F32 = jnp.float32
BF16 = jnp.bfloat16

D_MODEL = 2048
DA_HEADS = 4
DA_HEAD_DIM = 128
DA_VDIM = 256
DA_WIDTH = DA_HEADS * DA_VDIM
NA_HEADS = 8
NA_HEAD_DIM = 128
NA_WIDTH = NA_HEADS * NA_HEAD_DIM
GRID_W = 64
NA_ROWS = 8
NA_COLS = 16
D_FF = 4 * D_MODEL
ALIBI_MAX_BIAS = 8.0
EPS = 1e-6
IN_W = 3 * DA_WIDTH + 3 * NA_WIDTH + 2 * D_MODEL
LAMBDA_INIT = 0.8 - 0.6 * math.exp(-0.3 * 0)

COL_QA, COL_KA, COL_VA = 0, DA_WIDTH, 2 * DA_WIDTH
COL_QN = 3 * DA_WIDTH
COL_KN = COL_QN + NA_WIDTH
COL_VN = COL_KN + NA_WIDTH
COL_GA = COL_VN + NA_WIDTH
COL_GB = COL_GA + D_MODEL

NA_GROUP = 4
NA_KROWS = NA_GROUP + NA_ROWS
NA_Q = NA_GROUP * GRID_W
NA_K = NA_KROWS * GRID_W
NA_RPB_R = 2 * NA_ROWS - 1
NA_RPB_C = 2 * NA_COLS - 1
NEG = -1e30

VMEM_LIMIT = 56 * 1024 * 1024


def _params(sem):
    return pltpu.CompilerParams(dimension_semantics=sem, vmem_limit_bytes=VMEM_LIMIT)


def _ada_kernel(c_ref, w_ref, b_ref, o_ref):
    c = c_ref[...].astype(BF16)
    w = w_ref[...].astype(BF16)
    o_ref[...] = jnp.dot(c, w, preferred_element_type=F32) + b_ref[...]


def _ada(c_all, w_ada, b_ada, tn=1024):
    nb = c_all.shape[0]
    n = w_ada.shape[1]
    return pl.pallas_call(
        _ada_kernel,
        out_shape=jax.ShapeDtypeStruct((nb, n), F32),
        grid=(n // tn,),
        in_specs=[
            pl.BlockSpec((nb, D_MODEL), lambda j: (0, 0)),
            pl.BlockSpec((D_MODEL, tn), lambda j: (0, j)),
            pl.BlockSpec((1, tn), lambda j: (0, j)),
        ],
        out_specs=pl.BlockSpec((nb, tn), lambda j: (0, j)),
        compiler_params=_params(("parallel",)),
        name="ada_proj",
    )(c_all, w_ada, b_ada.reshape(1, n))


def _modulated_norm(x, g, sc, sh):
    ms = jnp.mean(x * x, axis=-1, keepdims=True)
    return ((x * lax.rsqrt(ms + EPS) * g) * (1.0 + sc) + sh).astype(BF16)


def _inproj_kernel(x_ref, g_ref, sc_ref, sh_ref, w_ref, cs_ref, o_ref, h_ref, *, rows):
    @pl.when(pl.program_id(2) == 0)
    def _():
        for c in range(x_ref.shape[1] // rows):
            sl = pl.ds(c * rows, rows)
            h = _modulated_norm(x_ref[0, sl, :], g_ref[...], sc_ref[0], sh_ref[0])
            h_ref[sl, :] = h
            o_ref[0, sl, :] = (jnp.dot(h, w_ref[...], preferred_element_type=F32) * cs_ref[...]).astype(BF16)

    @pl.when(pl.program_id(2) != 0)
    def _():
        o_ref[0] = (jnp.dot(h_ref[...], w_ref[...], preferred_element_type=F32) * cs_ref[...]).astype(BF16)


QUERY_SCALE = (DA_HEAD_DIM ** -0.5) * math.log2(math.e)


def _inproj(x, ada3, g_mix, w_in, tm=1024, tn=2048, rows=256):
    b, s, _ = x.shape
    assert DA_HEAD_DIM == NA_HEAD_DIM
    col_scale = np.ones((1, IN_W), np.float32)
    col_scale[0, COL_QA:COL_QA + DA_WIDTH] = QUERY_SCALE
    col_scale[0, COL_QN:COL_QN + NA_WIDTH] = QUERY_SCALE
    return pl.pallas_call(
        functools.partial(_inproj_kernel, rows=rows),
        out_shape=jax.ShapeDtypeStruct((b, s, IN_W), BF16),
        grid=(b, s // tm, IN_W // tn),
        in_specs=[
            pl.BlockSpec((1, tm, D_MODEL), lambda bi, i, n: (bi, i, 0)),
            pl.BlockSpec((1, D_MODEL), lambda bi, i, n: (0, 0)),
            pl.BlockSpec((1, 1, D_MODEL), lambda bi, i, n: (bi * 6 + 1, 0, 0)),
            pl.BlockSpec((1, 1, D_MODEL), lambda bi, i, n: (bi * 6 + 0, 0, 0)),
            pl.BlockSpec((D_MODEL, tn), lambda bi, i, n: (0, n)),
            pl.BlockSpec((1, tn), lambda bi, i, n: (0, n)),
        ],
        out_specs=pl.BlockSpec((1, tm, tn), lambda bi, i, n: (bi, i, n)),
        scratch_shapes=[pltpu.VMEM((tm, D_MODEL), BF16)],
        compiler_params=_params(("parallel", "parallel", "arbitrary")),
        name="in_proj",
    )(x, g_mix, ada3, ada3, w_in, jnp.asarray(col_scale))


DA_NCH = 4

def _alibi_lane_constants():
    csum = np.zeros((DA_HEADS,), np.float32)
    aq = np.zeros((DA_HEADS, DA_HEAD_DIM, 1), np.float32)
    ak = np.zeros((DA_HEADS, 1, DA_HEAD_DIM), np.float32)
    for h in range(DA_HEADS):
        slope = 2.0 ** (-ALIBI_MAX_BIAS * (h + 1) / DA_HEADS)
        rem = slope * math.log2(math.e)
        pieces = []
        for _ in range(DA_NCH):
            p = float(np.float32(rem).astype(ml_dtypes.bfloat16).astype(np.float32))
            pieces.append(p)
            rem -= p
        csum[h] = np.float32(sum(pieces))
        for d, weight in enumerate((256.0, 16.0, 1.0)):
            for n, p in enumerate(pieces):
                ak[h, 0, d * DA_NCH + n] = -weight * p
                aq[h, (3 + d) * DA_NCH + n, 0] = p
    return csum, aq, ak


def _da_kernel(csum_ref, lam_ref, aq_ref, ak_ref, q_ref, k_ref, v_ref, g_ref, o_ref,
               acc_ref, qt_ref, ka_ref, corr_ref, s_ref, *, tk):
    h = pl.program_id(0)
    i = pl.program_id(2)
    tq = q_ref.shape[1]
    s_len = k_ref.shape[1]
    n3 = 3 * DA_NCH

    lq = lam_ref[...]
    e1 = jnp.exp(jnp.sum(lq[0:1] * lq[1:2], axis=-1, keepdims=True))
    e2 = jnp.exp(jnp.sum(lq[2:3] * lq[3:4], axis=-1, keepdims=True))
    lam = e1 - e2 + LAMBDA_INIT

    @pl.when((i == 0) & (pl.program_id(1) == 0))
    def _():
        gap = (lax.broadcasted_iota(jnp.int32, (tk, tq), 0)
               - lax.broadcasted_iota(jnp.int32, (tk, tq), 1))
        for d in range(2):
            corr_ref[d] = jnp.maximum(gap + d * tk, 0).astype(F32) * (-2.0 * csum_ref[h])

    @pl.when(i == 0)
    def _():
        ck = ak_ref[0]
        lane = lax.broadcasted_iota(jnp.int32, (tk, DA_HEAD_DIM), 1)
        row = lax.broadcasted_iota(jnp.int32, (tk, DA_HEAD_DIM), 0)

        def build(r, carry):
            r0 = pl.multiple_of(r * tk, tk)
            pos = row + r * tk
            l2 = lane - n3
            dig = jnp.where(l2 < DA_NCH, (pos >> 8) << 8,
                            jnp.where(l2 < 2 * DA_NCH, ((pos >> 4) & 15) << 4, pos & 15)).astype(F32)
            kpos = jnp.where((lane >= n3) & (lane < 2 * n3), dig, ck).astype(BF16)
            for c in range(2):
                ka_ref[c, pl.ds(r0, tk), 0:DA_HEAD_DIM] = k_ref[0, pl.ds(r0, tk), c * DA_HEAD_DIM:(c + 1) * DA_HEAD_DIM]
                ka_ref[c, pl.ds(r0, tk), DA_HEAD_DIM:] = kpos
            return carry

        lax.fori_loop(0, s_len // tk, build, 0)

    prow = lax.broadcasted_iota(jnp.int32, (DA_HEAD_DIM, tq), 0)
    pos = lax.broadcasted_iota(jnp.int32, (DA_HEAD_DIM, tq), 1) + i * tq
    dig = jnp.where(prow < DA_NCH, pos >> 8, jnp.where(prow < 2 * DA_NCH, (pos >> 4) & 15, pos & 15)).astype(F32)
    qpos = jnp.where(prow < n3, dig, aq_ref[0])
    for c in range(2):
        qct = q_ref[0, :, c * DA_HEAD_DIM:(c + 1) * DA_HEAD_DIM].astype(F32).T.astype(BF16)
        for sign in range(2):
            qt_ref[sign, c, 0:DA_HEAD_DIM, :] = qct
            qt_ref[sign, c, DA_HEAD_DIM:, :] = (qpos if sign == 0 else -qpos).astype(BF16)
    acc_ref[...] = jnp.zeros_like(acc_ref)

    def scores(j, slot, above, diag=None):
        j0 = pl.multiple_of(j * tk, tk)
        maxes = []
        for c in range(2):
            s = jnp.dot(ka_ref[c, pl.ds(j0, tk), :], qt_ref[above, c], preferred_element_type=F32)
            if diag is not None:
                s = s + corr_ref[diag]
            s_ref[slot, c] = s
            maxes.append(jnp.max(s, axis=0, keepdims=True))
        return tuple(maxes)

    def consume(j, slot, maxes, stats):
        vc = v_ref[0, pl.ds(pl.multiple_of(j * tk, tk), tk), :]
        out = []
        for c in range(2):
            m, l = stats[2 * c], stats[2 * c + 1]
            m_new = jnp.maximum(m, maxes[c])
            alpha = jnp.exp2(m - m_new)
            p = jnp.exp2(s_ref[slot, c] - m_new)
            l_new = alpha * l + jnp.sum(p, axis=0, keepdims=True)
            acc_ref[c] = alpha * acc_ref[c] + lax.dot_general(
                vc, p.astype(BF16), (((0,), (0,)), ((), ())), preferred_element_type=F32)
            out += [m_new, l_new]
        return tuple(out)

    nk = s_len // tk

    def off_diagonal(u):
        above = (u >= 2 * i).astype(jnp.int32)
        return u + 2 * above, above

    m_init = jnp.full((1, tq), NEG, F32)
    l_init = jnp.zeros((1, tq), F32)
    mx_d0 = scores(2 * i, 0, 0, diag=0)
    mx_d1 = scores(2 * i + 1, 1, 0, diag=1)
    stats = consume(2 * i, 0, mx_d0, (m_init, l_init, m_init, l_init))

    def pair(p, carry):
        j_prev, mx_prev, stats = carry
        j_a, above = off_diagonal(2 * p)
        mx_a = scores(j_a, 0, above)
        stats = consume(j_prev, 1, mx_prev, stats)
        j_b, above = off_diagonal(2 * p + 1)
        mx_b = scores(j_b, 1, above)
        stats = consume(j_a, 0, mx_a, stats)
        return j_b, mx_b, stats

    trips = jnp.where(i >= 0, (nk - 2) // 2, 0)
    j_last, mx_last, stats = lax.fori_loop(0, trips, pair, (2 * i + 1, mx_d1, stats))
    m0, l0, m1, l1 = consume(j_last, 1, mx_last, stats)

    ot = acc_ref[0] / l0 - lam * (acc_ref[1] / l1)
    ms = jnp.mean(ot * ot, axis=0, keepdims=True)
    y = (ot * lax.rsqrt(ms + EPS)).T * g_ref[...] * (1.0 - LAMBDA_INIT)
    o_ref[0] = y.astype(BF16)


def _diff_attention(proj, lam4, g_subln, tq=1024):
    b, s, _ = proj.shape
    tk = tq // 2
    assert s % tq == 0 and s >= 2 * tq and s <= 16 ** 3
    qb, kb, vb = COL_QA // DA_VDIM, COL_KA // DA_VDIM, COL_VA // DA_VDIM
    csum, aq, ak = _alibi_lane_constants()
    return pl.pallas_call(
        functools.partial(_da_kernel, tk=tk),
        out_shape=jax.ShapeDtypeStruct((b, s, DA_WIDTH), BF16),
        grid=(DA_HEADS, b, s // tq),
        in_specs=[
            pl.BlockSpec(memory_space=pltpu.SMEM),
            pl.BlockSpec((4, DA_HEAD_DIM), lambda h, bi, i: (0, 0)),
            pl.BlockSpec((1, DA_HEAD_DIM, 1), lambda h, bi, i: (h, 0, 0)),
            pl.BlockSpec((1, 1, DA_HEAD_DIM), lambda h, bi, i: (h, 0, 0)),
            pl.BlockSpec((1, tq, DA_VDIM), lambda h, bi, i: (bi, i, qb + h)),
            pl.BlockSpec((1, s, DA_VDIM), lambda h, bi, i: (bi, 0, kb + h)),
            pl.BlockSpec((1, s, DA_VDIM), lambda h, bi, i: (bi, 0, vb + h)),
            pl.BlockSpec((1, DA_VDIM), lambda h, bi, i: (0, 0)),
        ],
        out_specs=pl.BlockSpec((1, tq, DA_VDIM), lambda h, bi, i: (bi, i, h)),
        scratch_shapes=[
            pltpu.VMEM((2, DA_VDIM, tq), F32),
            pltpu.VMEM((2, 2, 2 * DA_HEAD_DIM, tq), BF16),
            pltpu.VMEM((2, s, 2 * DA_HEAD_DIM), BF16),
            pltpu.VMEM((2, tk, tq), F32),
            pltpu.VMEM((2, 2, tk, tq), F32),
        ],
        compiler_params=_params(("parallel", "arbitrary", "arbitrary")),
        name="diff_attn",
    )(jnp.asarray(csum), lam4, jnp.asarray(aq), jnp.asarray(ak), proj, proj, proj, g_subln)


def _na_variant_rows(variant, ri, kr):
    if variant == 0:
        valid, dr = 0 <= kr < NA_ROWS, kr - ri + NA_ROWS - 1
    elif variant == 1:
        valid, dr = ri <= kr < ri + NA_ROWS, kr - ri + NA_ROWS - 1 - NA_GROUP
    else:
        valid, dr = NA_GROUP <= kr < NA_KROWS, kr - ri - 1
    return dr if valid else None


def _na_bias_kernel(rpb_ref, tab_ref, tt_ref):
    h = pl.program_id(0)
    shape = (GRID_W, 2 * GRID_W)
    lane = lax.broadcasted_iota(jnp.int32, shape, 1)
    kcol = lax.broadcasted_iota(jnp.int32, shape, 0)
    col = lane & (GRID_W - 1)
    rel = kcol - col + (NA_COLS - 1)
    cstart = jnp.clip(col - NA_COLS // 2, 0, GRID_W - NA_COLS)
    band = (kcol >= cstart) & (kcol < cstart + NA_COLS)
    neg = jnp.full(shape, NEG, F32)
    for dr in range(NA_RPB_R):
        t = neg
        for dc in range(NA_RPB_C):
            val = rpb_ref[h * (NA_RPB_R * NA_RPB_C) + dr * NA_RPB_C + dc]
            t = jnp.where(rel == dc, val, t)
        tt_ref[dr] = jnp.where(band, t * math.log2(math.e), neg)
    first_half = lane < GRID_W
    for variant in range(3):
        for kr in range(NA_KROWS):
            for pair in range(NA_GROUP // 2):
                da = _na_variant_rows(variant, 2 * pair, kr)
                db = _na_variant_rows(variant, 2 * pair + 1, kr)
                ta = neg if da is None else tt_ref[da]
                tb = neg if db is None else tt_ref[db]
                tab_ref[0, variant, kr * GRID_W:(kr + 1) * GRID_W,
                        pair * 2 * GRID_W:(pair + 1) * 2 * GRID_W] = jnp.where(first_half, ta, tb)


def _na_bias(rpb_flat):
    return pl.pallas_call(
        _na_bias_kernel,
        out_shape=jax.ShapeDtypeStruct((NA_HEADS, 3, NA_K, NA_Q), F32),
        grid=(NA_HEADS,),
        in_specs=[pl.BlockSpec(memory_space=pltpu.SMEM)],
        out_specs=pl.BlockSpec((1, 3, NA_K, NA_Q), lambda h: (h, 0, 0, 0)),
        scratch_shapes=[pltpu.VMEM((NA_RPB_R, GRID_W, 2 * GRID_W), F32)],
        compiler_params=_params(("arbitrary",)),
        name="na_bias",
    )(rpb_flat)


def _na_kernel(q_ref, k_ref, v_ref, tab_ref, o_ref, s_ref, *, n_rows):
    n_groups = n_rows // NA_GROUP
    n_blocks = NA_KROWS // NA_GROUP

    heads = range(q_ref.shape[2] // NA_HEAD_DIM)

    def lanes(hh):
        return slice(hh * NA_HEAD_DIM, (hh + 1) * NA_HEAD_DIM)

    def window_start(g):
        first_block = jnp.clip(g - 1, 0, n_groups - n_blocks)
        return pl.multiple_of(first_block * NA_Q, NA_Q)

    def scores(g, variant, slot):
        start = window_start(g)
        q0 = pl.multiple_of(g * NA_Q, NA_Q)
        maxes = []
        for hh in heads:
            kw = k_ref[0, pl.ds(start, NA_K), lanes(hh)]
            s = lax.dot_general(kw, q_ref[0, pl.ds(q0, NA_Q), lanes(hh)], (((1,), (1,)), ((), ())),
                                preferred_element_type=F32)
            s = s + tab_ref[hh, variant]
            s_ref[hh, slot] = s
            maxes.append(jnp.max(s, axis=0, keepdims=True))
        return tuple(maxes)

    def finish(g, slot, maxes):
        start = window_start(g)
        q0 = pl.multiple_of(g * NA_Q, NA_Q)
        for hh in heads:
            p = jnp.exp2(s_ref[hh, slot] - maxes[hh])
            l = jnp.sum(p, axis=0, keepdims=True)
            vw = v_ref[0, pl.ds(start, NA_K), lanes(hh)]
            ot = lax.dot_general(vw, p.astype(BF16), (((0,), (0,)), ((), ())), preferred_element_type=F32)
            o_ref[0, pl.ds(q0, NA_Q), lanes(hh)] = (ot / l).T.astype(BF16)

    m_first = scores(0, 0, 0)

    def pair(t, carry):
        g_prev, m_prev = carry
        g_a = 1 + 2 * t
        m_a = scores(g_a, 1, 1)
        finish(g_prev, 0, m_prev)
        g_b = g_a + 1
        m_b = scores(g_b, 1, 0)
        finish(g_a, 1, m_a)
        return g_b, m_b

    g_prev, m_prev = lax.fori_loop(0, (n_groups - 2) // 2, pair, (0, m_first))
    m_last = scores(n_groups - 1, 2, 1)
    finish(g_prev, 0, m_prev)
    finish(n_groups - 1, 1, m_last)


def _neighborhood_attention(proj, table):
    b, s, _ = proj.shape
    n_rows = s // GRID_W
    assert n_rows % NA_GROUP == 0 and n_rows >= NA_KROWS and NA_ROWS // 2 == NA_GROUP
    assert NA_KROWS % NA_GROUP == 0 and (n_rows // NA_GROUP) % 2 == 0
    hpb = 4
    width = hpb * NA_HEAD_DIM
    assert NA_HEADS % hpb == 0
    qb, kb, vb = COL_QN // width, COL_KN // width, COL_VN // width
    return pl.pallas_call(
        functools.partial(_na_kernel, n_rows=n_rows),
        out_shape=jax.ShapeDtypeStruct((b, s, NA_WIDTH), BF16),
        grid=(NA_HEADS // hpb, b),
        in_specs=[
            pl.BlockSpec((1, s, width), lambda h, bi: (bi, 0, qb + h)),
            pl.BlockSpec((1, s, width), lambda h, bi: (bi, 0, kb + h)),
            pl.BlockSpec((1, s, width), lambda h, bi: (bi, 0, vb + h)),
            pl.BlockSpec((hpb, 3, NA_K, NA_Q), lambda h, bi: (h, 0, 0, 0), pipeline_mode=pl.Buffered(1)),
        ],
        out_specs=pl.BlockSpec((1, s, width), lambda h, bi: (bi, 0, h)),
        scratch_shapes=[pltpu.VMEM((hpb, 2, NA_K, NA_Q), F32)],
        compiler_params=_params(("parallel", "parallel")),
        name="nbr_attn",
    )(proj, proj, proj, table)


def _sigmoid(x):
    return 1.0 / (1.0 + jnp.exp(-x))


def _merge_kernel(x_ref, oa_ref, on_ref, ga_ref, gb_ref, wpa_ref, wpb_ref, wout_ref,
                  gt1_ref, gmlp_ref, sc2_ref, sh2_ref, x1_ref, h2_ref, *, parts):
    rows = x_ref.shape[1] // parts
    gain2 = gmlp_ref[...] * (1.0 + sc2_ref[0])
    for r in range(parts):
        sl = pl.ds(r * rows, rows)
        pa = jnp.dot(oa_ref[0, sl, :], wpa_ref[...], preferred_element_type=F32)
        pb = jnp.dot(on_ref[0, sl, :], wpb_ref[...], preferred_element_type=F32)
        merged = (_sigmoid(ga_ref[0, sl, :].astype(F32)) * pa
                  + _sigmoid(gb_ref[0, sl, :].astype(F32)) * pb).astype(BF16)
        out = jnp.dot(merged, wout_ref[...], preferred_element_type=F32)
        x1 = x_ref[0, sl, :] + gt1_ref[0] * out
        x1_ref[0, sl, :] = x1
        ms = jnp.mean(x1 * x1, axis=-1, keepdims=True)
        h2_ref[0, sl, :] = (x1 * lax.rsqrt(ms + EPS) * gain2 + sh2_ref[0]).astype(BF16)


def _merge(x, oa, on, proj, w_pa, w_pb, w_out, ada3, g_mlp, tm=512):
    b, s, _ = x.shape
    ga_b, gb_b = COL_GA // D_MODEL, COL_GB // D_MODEL
    once = pl.Buffered(1)

    def ada_spec(idx):
        return pl.BlockSpec((1, 1, D_MODEL), lambda bi, i: (bi * 6 + idx, 0, 0))

    return pl.pallas_call(
        functools.partial(_merge_kernel, parts=2),
        out_shape=(jax.ShapeDtypeStruct((b, s, D_MODEL), F32),
                   jax.ShapeDtypeStruct((b, s, D_MODEL), BF16)),
        grid=(b, s // tm),
        in_specs=[
            pl.BlockSpec((1, tm, D_MODEL), lambda bi, i: (bi, i, 0)),
            pl.BlockSpec((1, tm, DA_WIDTH), lambda bi, i: (bi, i, 0)),
            pl.BlockSpec((1, tm, NA_WIDTH), lambda bi, i: (bi, i, 0)),
            pl.BlockSpec((1, tm, D_MODEL), lambda bi, i: (bi, i, ga_b)),
            pl.BlockSpec((1, tm, D_MODEL), lambda bi, i: (bi, i, gb_b)),
            pl.BlockSpec((DA_WIDTH, D_MODEL), lambda bi, i: (0, 0), pipeline_mode=once),
            pl.BlockSpec((NA_WIDTH, D_MODEL), lambda bi, i: (0, 0), pipeline_mode=once),
            pl.BlockSpec((D_MODEL, D_MODEL), lambda bi, i: (0, 0), pipeline_mode=once),
            ada_spec(2),
            pl.BlockSpec((1, D_MODEL), lambda bi, i: (0, 0)),
            ada_spec(4),
            ada_spec(3),
        ],
        out_specs=(pl.BlockSpec((1, tm, D_MODEL), lambda bi, i: (bi, i, 0)),
                   pl.BlockSpec((1, tm, D_MODEL), lambda bi, i: (bi, i, 0))),
        compiler_params=_params(("parallel", "parallel")),
        name="merge_out_proj",
    )(x, oa, on, proj, proj, w_pa, w_pb, w_out, ada3, g_mlp, ada3, ada3)


def _mlp_kernel(h2_ref, w1_ref, w2_ref, x1_ref, gt2_ref, gf_ref, y_ref, *, rows):
    f = pl.program_id(2)
    last = pl.num_programs(2) - 1

    def contribution(sl):
        u = jnp.dot(h2_ref[0, sl, :], w1_ref[...], preferred_element_type=F32)
        u = jnp.square(jnp.maximum(u, 0.0)).astype(BF16)
        return jnp.dot(u, w2_ref[...], preferred_element_type=F32)

    @pl.when(f == 0)
    def _():
        y_ref[0] = contribution(slice(None))

    @pl.when((f != 0) & (f != last))
    def _():
        y_ref[0] += contribution(slice(None))

    @pl.when(f == last)
    def _():
        for c in range(y_ref.shape[1] // rows):
            sl = pl.ds(c * rows, rows)
            x2 = x1_ref[0, sl, :] + gt2_ref[0] * (y_ref[0, sl, :] + contribution(sl))
            ms = jnp.mean(x2 * x2, axis=-1, keepdims=True)
            y_ref[0, sl, :] = x2 * lax.rsqrt(ms + EPS) * gf_ref[...]


def _mlp(h2, x1, w1, w2, ada3, g_final, tm=512, tf=1024):
    b, s, _ = x1.shape
    return pl.pallas_call(
        functools.partial(_mlp_kernel, rows=tm // 2),
        out_shape=jax.ShapeDtypeStruct((b, s, D_MODEL), F32),
        grid=(b, s // tm, D_FF // tf),
        in_specs=[
            pl.BlockSpec((1, tm, D_MODEL), lambda bi, i, f: (bi, i, 0)),
            pl.BlockSpec((D_MODEL, tf), lambda bi, i, f: (0, f)),
            pl.BlockSpec((tf, D_MODEL), lambda bi, i, f: (f, 0)),
            pl.BlockSpec((1, tm, D_MODEL), lambda bi, i, f: (bi, i, 0)),
            pl.BlockSpec((1, 1, D_MODEL), lambda bi, i, f: (bi * 6 + 5, 0, 0)),
            pl.BlockSpec((1, D_MODEL), lambda bi, i, f: (0, 0)),
        ],
        out_specs=pl.BlockSpec((1, tm, D_MODEL), lambda bi, i, f: (bi, i, 0)),
        compiler_params=_params(("parallel", "parallel", "arbitrary")),
        name="mlp_final_norm",
    )(h2, w1, w2, x1, ada3, g_final)


def _trunk(x, ada, w, table, lam4):
    b = x.shape[0]
    ada3 = ada.reshape(b * 6, 1, D_MODEL)
    proj = _inproj(x, ada3, w["g_mix"], w["w_in"])
    oa = _diff_attention(proj, lam4, w["g_subln"])
    on = _neighborhood_attention(proj, table)
    x1, h2 = _merge(x, oa, on, proj, w["w_pa"], w["w_pb"], w["w_out"], ada3, w["g_mlp"])
    return _mlp(h2, x1, w["w1"], w["w2"], ada3, w["g_final"])


def kernel(x_prompt, x_sample, c_prompt, c_sample, w_ada, b_ada, g_mix, w_in, lam_q1, lam_k1, lam_q2, lam_k2, g_subln, rpb, w_pa, w_pb, w_out, g_mlp, w1, w2, g_final):
    w = {
        "g_mix": g_mix[0].reshape(1, D_MODEL),
        "w_in": w_in[0].astype(BF16),
        "g_subln": g_subln[0].reshape(1, DA_VDIM),
        "w_pa": w_pa[0].astype(BF16),
        "w_pb": w_pb[0].astype(BF16),
        "w_out": w_out[0].astype(BF16),
        "g_mlp": g_mlp[0].reshape(1, D_MODEL),
        "w1": w1[0].astype(BF16),
        "w2": w2[0].astype(BF16),
        "g_final": g_final.reshape(1, D_MODEL),
    }
    nb_p = c_prompt.shape[0]
    ada = _ada(jnp.concatenate([c_prompt, c_sample], axis=0), w_ada[0], b_ada[0])
    table = _na_bias(rpb[0].reshape(-1))
    lam4 = jnp.stack([lam_q1[0], lam_k1[0], lam_q2[0], lam_k2[0]], axis=0)
    y_prompt = _trunk(x_prompt, ada[:nb_p], w, table, lam4)
    y_sample = _trunk(x_sample, ada[nb_p:], w, table, lam4)
    return (y_prompt, y_sample)
```

```python
import functools
import math

import jax
import jax.numpy as jnp
import ml_dtypes
import numpy as np
from jax import lax
from jax.experimental import pallas as pl
from jax.experimental.pallas import tpu as pltpu

F32 = jnp.float32
BF16 = jnp.bfloat16

D_MODEL = 2048
DA_HEADS = 4
DA_HEAD_DIM = 128
DA_VDIM = 256
DA_WIDTH = DA_HEADS * DA_VDIM
NA_HEADS = 8
NA_HEAD_DIM = 128
NA_WIDTH = NA_HEADS * NA_HEAD_DIM
GRID_W = 64
NA_ROWS = 8
NA_COLS = 16
D_FF = 4 * D_MODEL
ALIBI_MAX_BIAS = 8.0
EPS = 1e-6
IN_W = 3 * DA_WIDTH + 3 * NA_WIDTH + 2 * D_MODEL
LAMBDA_INIT = 0.8 - 0.6 * math.exp(-0.3 * 0)

COL_QA, COL_KA, COL_VA = 0, DA_WIDTH, 2 * DA_WIDTH
COL_QN = 3 * DA_WIDTH
COL_KN = COL_QN + NA_WIDTH
COL_VN = COL_KN + NA_WIDTH
COL_GA = COL_VN + NA_WIDTH
COL_GB = COL_GA + D_MODEL

NA_GROUP = 4
NA_KROWS = NA_GROUP + NA_ROWS
NA_Q = NA_GROUP * GRID_W
NA_K = NA_KROWS * GRID_W
NA_RPB_R = 2 * NA_ROWS - 1
NA_RPB_C = 2 * NA_COLS - 1
NEG = -1e30

VMEM_LIMIT = 56 * 1024 * 1024


def _params(sem):
    return pltpu.CompilerParams(dimension_semantics=sem, vmem_limit_bytes=VMEM_LIMIT)


def _ada_kernel(c_ref, w_ref, b_ref, o_ref):
    c = c_ref[...].astype(BF16)
    w = w_ref[...].astype(BF16)
    o_ref[...] = jnp.dot(c, w, preferred_element_type=F32) + b_ref[...]


def _ada(c_all, w_ada, b_ada, tn=1024):
    nb = c_all.shape[0]
    n = w_ada.shape[1]
    return pl.pallas_call(
        _ada_kernel,
        out_shape=jax.ShapeDtypeStruct((nb, n), F32),
        grid=(n // tn,),
        in_specs=[
            pl.BlockSpec((nb, D_MODEL), lambda j: (0, 0)),
            pl.BlockSpec((D_MODEL, tn), lambda j: (0, j)),
            pl.BlockSpec((1, tn), lambda j: (0, j)),
        ],
        out_specs=pl.BlockSpec((nb, tn), lambda j: (0, j)),
        compiler_params=_params(("parallel",)),
        name="ada_proj",
    )(c_all, w_ada, b_ada.reshape(1, n))


def _modulated_norm(x, g, sc, sh):
    ms = jnp.mean(x * x, axis=-1, keepdims=True)
    return ((x * lax.rsqrt(ms + EPS) * g) * (1.0 + sc) + sh).astype(BF16)


def _inproj_kernel(x_ref, g_ref, sc_ref, sh_ref, w_ref, cs_ref, o_ref, h_ref, *, rows):
    @pl.when(pl.program_id(2) == 0)
    def _():
        for c in range(x_ref.shape[1] // rows):
            sl = pl.ds(c * rows, rows)
            h = _modulated_norm(x_ref[0, sl, :], g_ref[...], sc_ref[0], sh_ref[0])
            h_ref[sl, :] = h
            o_ref[0, sl, :] = (jnp.dot(h, w_ref[...], preferred_element_type=F32) * cs_ref[...]).astype(BF16)

    @pl.when(pl.program_id(2) != 0)
    def _():
        o_ref[0] = (jnp.dot(h_ref[...], w_ref[...], preferred_element_type=F32) * cs_ref[...]).astype(BF16)


QUERY_SCALE = (DA_HEAD_DIM ** -0.5) * math.log2(math.e)


def _inproj(x, ada3, g_mix, w_in, tm=1024, tn=2048, rows=256):
    b, s, _ = x.shape
    assert DA_HEAD_DIM == NA_HEAD_DIM
    col_scale = np.ones((1, IN_W), np.float32)
    col_scale[0, COL_QA:COL_QA + DA_WIDTH] = QUERY_SCALE
    col_scale[0, COL_QN:COL_QN + NA_WIDTH] = QUERY_SCALE
    return pl.pallas_call(
        functools.partial(_inproj_kernel, rows=rows),
        out_shape=jax.ShapeDtypeStruct((b, s, IN_W), BF16),
        grid=(b, s // tm, IN_W // tn),
        in_specs=[
            pl.BlockSpec((1, tm, D_MODEL), lambda bi, i, n: (bi, i, 0)),
            pl.BlockSpec((1, D_MODEL), lambda bi, i, n: (0, 0)),
            pl.BlockSpec((1, 1, D_MODEL), lambda bi, i, n: (bi * 6 + 1, 0, 0)),
            pl.BlockSpec((1, 1, D_MODEL), lambda bi, i, n: (bi * 6 + 0, 0, 0)),
            pl.BlockSpec((D_MODEL, tn), lambda bi, i, n: (0, n)),
            pl.BlockSpec((1, tn), lambda bi, i, n: (0, n)),
        ],
        out_specs=pl.BlockSpec((1, tm, tn), lambda bi, i, n: (bi, i, n)),
        scratch_shapes=[pltpu.VMEM((tm, D_MODEL), BF16)],
        compiler_params=_params(("parallel", "parallel", "arbitrary")),
        name="in_proj",
    )(x, g_mix, ada3, ada3, w_in, jnp.asarray(col_scale))


DA_NCH = 4

def _alibi_lane_constants():
    csum = np.zeros((DA_HEADS,), np.float32)
    aq = np.zeros((DA_HEADS, DA_HEAD_DIM, 1), np.float32)
    ak = np.zeros((DA_HEADS, 1, DA_HEAD_DIM), np.float32)
    for h in range(DA_HEADS):
        slope = 2.0 ** (-ALIBI_MAX_BIAS * (h + 1) / DA_HEADS)
        rem = slope * math.log2(math.e)
        pieces = []
        for _ in range(DA_NCH):
            p = float(np.float32(rem).astype(ml_dtypes.bfloat16).astype(np.float32))
            pieces.append(p)
            rem -= p
        csum[h] = np.float32(sum(pieces))
        for d, weight in enumerate((256.0, 16.0, 1.0)):
            for n, p in enumerate(pieces):
                ak[h, 0, d * DA_NCH + n] = -weight * p
                aq[h, (3 + d) * DA_NCH + n, 0] = p
    return csum, aq, ak


def _da_kernel(csum_ref, lam_ref, aq_ref, ak_ref, q_ref, k_ref, v_ref, g_ref, o_ref,
               acc_ref, qt_ref, ka_ref, corr_ref, s_ref, *, tk):
    h = pl.program_id(0)
    i = pl.program_id(2)
    tq = q_ref.shape[1]
    s_len = k_ref.shape[1]
    n3 = 3 * DA_NCH

    lq = lam_ref[...]
    e1 = jnp.exp(jnp.sum(lq[0:1] * lq[1:2], axis=-1, keepdims=True))
    e2 = jnp.exp(jnp.sum(lq[2:3] * lq[3:4], axis=-1, keepdims=True))
    lam = e1 - e2 + LAMBDA_INIT

    @pl.when((i == 0) & (pl.program_id(1) == 0))
    def _():
        gap = (lax.broadcasted_iota(jnp.int32, (tk, tq), 0)
               - lax.broadcasted_iota(jnp.int32, (tk, tq), 1))
        for d in range(2):
            corr_ref[d] = jnp.maximum(gap + d * tk, 0).astype(F32) * (-2.0 * csum_ref[h])

    @pl.when(i == 0)
    def _():
        ck = ak_ref[0]
        lane = lax.broadcasted_iota(jnp.int32, (tk, DA_HEAD_DIM), 1)
        row = lax.broadcasted_iota(jnp.int32, (tk, DA_HEAD_DIM), 0)

        def build(r, carry):
            r0 = pl.multiple_of(r * tk, tk)
            pos = row + r * tk
            l2 = lane - n3
            dig = jnp.where(l2 < DA_NCH, (pos >> 8) << 8,
                            jnp.where(l2 < 2 * DA_NCH, ((pos >> 4) & 15) << 4, pos & 15)).astype(F32)
            kpos = jnp.where((lane >= n3) & (lane < 2 * n3), dig, ck).astype(BF16)
            for c in range(2):
                ka_ref[c, pl.ds(r0, tk), 0:DA_HEAD_DIM] = k_ref[0, pl.ds(r0, tk), c * DA_HEAD_DIM:(c + 1) * DA_HEAD_DIM]
                ka_ref[c, pl.ds(r0, tk), DA_HEAD_DIM:] = kpos
            return carry

        lax.fori_loop(0, s_len // tk, build, 0)

    prow = lax.broadcasted_iota(jnp.int32, (DA_HEAD_DIM, tq), 0)
    pos = lax.broadcasted_iota(jnp.int32, (DA_HEAD_DIM, tq), 1) + i * tq
    dig = jnp.where(prow < DA_NCH, pos >> 8, jnp.where(prow < 2 * DA_NCH, (pos >> 4) & 15, pos & 15)).astype(F32)
    qpos = jnp.where(prow < n3, dig, aq_ref[0])
    for c in range(2):
        qct = q_ref[0, :, c * DA_HEAD_DIM:(c + 1) * DA_HEAD_DIM].astype(F32).T.astype(BF16)
        for sign in range(2):
            qt_ref[sign, c, 0:DA_HEAD_DIM, :] = qct
            qt_ref[sign, c, DA_HEAD_DIM:, :] = (qpos if sign == 0 else -qpos).astype(BF16)
    acc_ref[...] = jnp.zeros_like(acc_ref)

    def scores(j, slot, above, diag=None):
        j0 = pl.multiple_of(j * tk, tk)
        maxes = []
        for c in range(2):
            s = jnp.dot(ka_ref[c, pl.ds(j0, tk), :], qt_ref[above, c], preferred_element_type=F32)
            if diag is not None:
                s = s + corr_ref[diag]
            s_ref[slot, c] = s
            maxes.append(jnp.max(s, axis=0, keepdims=True))
        return tuple(maxes)

    def consume(j, slot, maxes, stats):
        vc = v_ref[0, pl.ds(pl.multiple_of(j * tk, tk), tk), :]
        out = []
        for c in range(2):
            m, l = stats[2 * c], stats[2 * c + 1]
            m_new = jnp.maximum(m, maxes[c])
            alpha = jnp.exp2(m - m_new)
            p = jnp.exp2(s_ref[slot, c] - m_new)
            l_new = alpha * l + jnp.sum(p, axis=0, keepdims=True)
            acc_ref[c] = alpha * acc_ref[c] + lax.dot_general(
                vc, p.astype(BF16), (((0,), (0,)), ((), ())), preferred_element_type=F32)
            out += [m_new, l_new]
        return tuple(out)

    nk = s_len // tk

    def off_diagonal(u):
        above = (u >= 2 * i).astype(jnp.int32)
        return u + 2 * above, above

    m_init = jnp.full((1, tq), NEG, F32)
    l_init = jnp.zeros((1, tq), F32)
    mx_d0 = scores(2 * i, 0, 0, diag=0)
    mx_d1 = scores(2 * i + 1, 1, 0, diag=1)
    stats = consume(2 * i, 0, mx_d0, (m_init, l_init, m_init, l_init))

    def pair(p, carry):
        j_prev, mx_prev, stats = carry
        j_a, above = off_diagonal(2 * p)
        mx_a = scores(j_a, 0, above)
        stats = consume(j_prev, 1, mx_prev, stats)
        j_b, above = off_diagonal(2 * p + 1)
        mx_b = scores(j_b, 1, above)
        stats = consume(j_a, 0, mx_a, stats)
        return j_b, mx_b, stats

    trips = jnp.where(i >= 0, (nk - 2) // 2, 0)
    j_last, mx_last, stats = lax.fori_loop(0, trips, pair, (2 * i + 1, mx_d1, stats))
    m0, l0, m1, l1 = consume(j_last, 1, mx_last, stats)

    ot = acc_ref[0] / l0 - lam * (acc_ref[1] / l1)
    ms = jnp.mean(ot * ot, axis=0, keepdims=True)
    y = (ot * lax.rsqrt(ms + EPS)).T * g_ref[...] * (1.0 - LAMBDA_INIT)
    o_ref[0] = y.astype(BF16)


def _diff_attention(proj, lam4, g_subln, tq=1024):
    b, s, _ = proj.shape
    tk = tq // 2
    assert s % tq == 0 and s >= 2 * tq and s <= 16 ** 3
    qb, kb, vb = COL_QA // DA_VDIM, COL_KA // DA_VDIM, COL_VA // DA_VDIM
    csum, aq, ak = _alibi_lane_constants()
    return pl.pallas_call(
        functools.partial(_da_kernel, tk=tk),
        out_shape=jax.ShapeDtypeStruct((b, s, DA_WIDTH), BF16),
        grid=(DA_HEADS, b, s // tq),
        in_specs=[
            pl.BlockSpec(memory_space=pltpu.SMEM),
            pl.BlockSpec((4, DA_HEAD_DIM), lambda h, bi, i: (0, 0)),
            pl.BlockSpec((1, DA_HEAD_DIM, 1), lambda h, bi, i: (h, 0, 0)),
            pl.BlockSpec((1, 1, DA_HEAD_DIM), lambda h, bi, i: (h, 0, 0)),
            pl.BlockSpec((1, tq, DA_VDIM), lambda h, bi, i: (bi, i, qb + h)),
            pl.BlockSpec((1, s, DA_VDIM), lambda h, bi, i: (bi, 0, kb + h)),
            pl.BlockSpec((1, s, DA_VDIM), lambda h, bi, i: (bi, 0, vb + h)),
            pl.BlockSpec((1, DA_VDIM), lambda h, bi, i: (0, 0)),
        ],
        out_specs=pl.BlockSpec((1, tq, DA_VDIM), lambda h, bi, i: (bi, i, h)),
        scratch_shapes=[
            pltpu.VMEM((2, DA_VDIM, tq), F32),
            pltpu.VMEM((2, 2, 2 * DA_HEAD_DIM, tq), BF16),
            pltpu.VMEM((2, s, 2 * DA_HEAD_DIM), BF16),
            pltpu.VMEM((2, tk, tq), F32),
            pltpu.VMEM((2, 2, tk, tq), F32),
        ],
        compiler_params=_params(("parallel", "arbitrary", "arbitrary")),
        name="diff_attn",
    )(jnp.asarray(csum), lam4, jnp.asarray(aq), jnp.asarray(ak), proj, proj, proj, g_subln)


def _na_variant_rows(variant, ri, kr):
    if variant == 0:
        valid, dr = 0 <= kr < NA_ROWS, kr - ri + NA_ROWS - 1
    elif variant == 1:
        valid, dr = ri <= kr < ri + NA_ROWS, kr - ri + NA_ROWS - 1 - NA_GROUP
    else:
        valid, dr = NA_GROUP <= kr < NA_KROWS, kr - ri - 1
    return dr if valid else None


def _na_bias_kernel(rpb_ref, tab_ref, tt_ref):
    h = pl.program_id(0)
    shape = (GRID_W, 2 * GRID_W)
    lane = lax.broadcasted_iota(jnp.int32, shape, 1)
    kcol = lax.broadcasted_iota(jnp.int32, shape, 0)
    col = lane & (GRID_W - 1)
    rel = kcol - col + (NA_COLS - 1)
    cstart = jnp.clip(col - NA_COLS // 2, 0, GRID_W - NA_COLS)
    band = (kcol >= cstart) & (kcol < cstart + NA_COLS)
    neg = jnp.full(shape, NEG, F32)
    for dr in range(NA_RPB_R):
        t = neg
        for dc in range(NA_RPB_C):
            val = rpb_ref[h * (NA_RPB_R * NA_RPB_C) + dr * NA_RPB_C + dc]
            t = jnp.where(rel == dc, val, t)
        tt_ref[dr] = jnp.where(band, t * math.log2(math.e), neg)
    first_half = lane < GRID_W
    for variant in range(3):
        for kr in range(NA_KROWS):
            for pair in range(NA_GROUP // 2):
                da = _na_variant_rows(variant, 2 * pair, kr)
                db = _na_variant_rows(variant, 2 * pair + 1, kr)
                ta = neg if da is None else tt_ref[da]
                tb = neg if db is None else tt_ref[db]
                tab_ref[0, variant, kr * GRID_W:(kr + 1) * GRID_W,
                        pair * 2 * GRID_W:(pair + 1) * 2 * GRID_W] = jnp.where(first_half, ta, tb)


def _na_bias(rpb_flat):
    return pl.pallas_call(
        _na_bias_kernel,
        out_shape=jax.ShapeDtypeStruct((NA_HEADS, 3, NA_K, NA_Q), F32),
        grid=(NA_HEADS,),
        in_specs=[pl.BlockSpec(memory_space=pltpu.SMEM)],
        out_specs=pl.BlockSpec((1, 3, NA_K, NA_Q), lambda h: (h, 0, 0, 0)),
        scratch_shapes=[pltpu.VMEM((NA_RPB_R, GRID_W, 2 * GRID_W), F32)],
        compiler_params=_params(("arbitrary",)),
        name="na_bias",
    )(rpb_flat)


def _na_kernel(q_ref, k_ref, v_ref, tab_ref, o_ref, s_ref, *, n_rows):
    n_groups = n_rows // NA_GROUP
    n_blocks = NA_KROWS // NA_GROUP

    heads = range(q_ref.shape[2] // NA_HEAD_DIM)

    def lanes(hh):
        return slice(hh * NA_HEAD_DIM, (hh + 1) * NA_HEAD_DIM)

    def window_start(g):
        first_block = jnp.clip(g - 1, 0, n_groups - n_blocks)
        return pl.multiple_of(first_block * NA_Q, NA_Q)

    def scores(g, variant, slot):
        start = window_start(g)
        q0 = pl.multiple_of(g * NA_Q, NA_Q)
        maxes = []
        for hh in heads:
            kw = k_ref[0, pl.ds(start, NA_K), lanes(hh)]
            s = lax.dot_general(kw, q_ref[0, pl.ds(q0, NA_Q), lanes(hh)], (((1,), (1,)), ((), ())),
                                preferred_element_type=F32)
            s = s + tab_ref[hh, variant]
            s_ref[hh, slot] = s
            maxes.append(jnp.max(s, axis=0, keepdims=True))
        return tuple(maxes)

    def finish(g, slot, maxes):
        start = window_start(g)
        q0 = pl.multiple_of(g * NA_Q, NA_Q)
        for hh in heads:
            p = jnp.exp2(s_ref[hh, slot] - maxes[hh])
            l = jnp.sum(p, axis=0, keepdims=True)
            vw = v_ref[0, pl.ds(start, NA_K), lanes(hh)]
            ot = lax.dot_general(vw, p.astype(BF16), (((0,), (0,)), ((), ())), preferred_element_type=F32)
            o_ref[0, pl.ds(q0, NA_Q), lanes(hh)] = (ot / l).T.astype(BF16)

    m_first = scores(0, 0, 0)

    def pair(t, carry):
        g_prev, m_prev = carry
        g_a = 1 + 2 * t
        m_a = scores(g_a, 1, 1)
        finish(g_prev, 0, m_prev)
        g_b = g_a + 1
        m_b = scores(g_b, 1, 0)
        finish(g_a, 1, m_a)
        return g_b, m_b

    g_prev, m_prev = lax.fori_loop(0, (n_groups - 2) // 2, pair, (0, m_first))
    m_last = scores(n_groups - 1, 2, 1)
    finish(g_prev, 0, m_prev)
    finish(n_groups - 1, 1, m_last)


def _neighborhood_attention(proj, table):
    b, s, _ = proj.shape
    n_rows = s // GRID_W
    assert n_rows % NA_GROUP == 0 and n_rows >= NA_KROWS and NA_ROWS // 2 == NA_GROUP
    assert NA_KROWS % NA_GROUP == 0 and (n_rows // NA_GROUP) % 2 == 0
    hpb = 4
    width = hpb * NA_HEAD_DIM
    assert NA_HEADS % hpb == 0
    qb, kb, vb = COL_QN // width, COL_KN // width, COL_VN // width
    return pl.pallas_call(
        functools.partial(_na_kernel, n_rows=n_rows),
        out_shape=jax.ShapeDtypeStruct((b, s, NA_WIDTH), BF16),
        grid=(NA_HEADS // hpb, b),
        in_specs=[
            pl.BlockSpec((1, s, width), lambda h, bi: (bi, 0, qb + h)),
            pl.BlockSpec((1, s, width), lambda h, bi: (bi, 0, kb + h)),
            pl.BlockSpec((1, s, width), lambda h, bi: (bi, 0, vb + h)),
            pl.BlockSpec((hpb, 3, NA_K, NA_Q), lambda h, bi: (h, 0, 0, 0), pipeline_mode=pl.Buffered(1)),
        ],
        out_specs=pl.BlockSpec((1, s, width), lambda h, bi: (bi, 0, h)),
        scratch_shapes=[pltpu.VMEM((hpb, 2, NA_K, NA_Q), F32)],
        compiler_params=_params(("parallel", "parallel")),
        name="nbr_attn",
    )(proj, proj, proj, table)


def _sigmoid(x):
    return 1.0 / (1.0 + jnp.exp(-x))


def _merge_kernel(x_ref, oa_ref, on_ref, ga_ref, gb_ref, wpa_ref, wpb_ref, wout_ref,
                  gt1_ref, x1_ref, *, parts):
    rows = x_ref.shape[1] // parts
    for r in range(parts):
        sl = pl.ds(r * rows, rows)
        pa = jnp.dot(oa_ref[0, sl, :], wpa_ref[...], preferred_element_type=F32)
        pb = jnp.dot(on_ref[0, sl, :], wpb_ref[...], preferred_element_type=F32)
        merged = (_sigmoid(ga_ref[0, sl, :].astype(F32)) * pa
                  + _sigmoid(gb_ref[0, sl, :].astype(F32)) * pb).astype(BF16)
        out = jnp.dot(merged, wout_ref[...], preferred_element_type=F32)
        x1_ref[0, sl, :] = x_ref[0, sl, :] + gt1_ref[0] * out


def _merge(x, oa, on, proj, w_pa, w_pb, w_out, ada3, tm=512):
    b, s, _ = x.shape
    ga_b, gb_b = COL_GA // D_MODEL, COL_GB // D_MODEL
    once = pl.Buffered(1)

    def ada_spec(idx):
        return pl.BlockSpec((1, 1, D_MODEL), lambda bi, i: (bi * 6 + idx, 0, 0))

    return pl.pallas_call(
        functools.partial(_merge_kernel, parts=2),
        out_shape=jax.ShapeDtypeStruct((b, s, D_MODEL), F32),
        grid=(b, s // tm),
        in_specs=[
            pl.BlockSpec((1, tm, D_MODEL), lambda bi, i: (bi, i, 0)),
            pl.BlockSpec((1, tm, DA_WIDTH), lambda bi, i: (bi, i, 0)),
            pl.BlockSpec((1, tm, NA_WIDTH), lambda bi, i: (bi, i, 0)),
            pl.BlockSpec((1, tm, D_MODEL), lambda bi, i: (bi, i, ga_b)),
            pl.BlockSpec((1, tm, D_MODEL), lambda bi, i: (bi, i, gb_b)),
            pl.BlockSpec((DA_WIDTH, D_MODEL), lambda bi, i: (0, 0), pipeline_mode=once),
            pl.BlockSpec((NA_WIDTH, D_MODEL), lambda bi, i: (0, 0), pipeline_mode=once),
            pl.BlockSpec((D_MODEL, D_MODEL), lambda bi, i: (0, 0), pipeline_mode=once),
            ada_spec(2),
        ],
        out_specs=pl.BlockSpec((1, tm, D_MODEL), lambda bi, i: (bi, i, 0)),
        compiler_params=_params(("parallel", "parallel")),
        name="merge_out_proj",
    )(x, oa, on, proj, proj, w_pa, w_pb, w_out, ada3)


def _mlp_kernel(w1_ref, w2_ref, x1_ref, gmlp_ref, sc2_ref, sh2_ref, gt2_ref, gf_ref, y_ref, h2_ref, *, rows):
    f = pl.program_id(2)
    last = pl.num_programs(2) - 1

    def contribution(sl):
        u = jnp.dot(h2_ref[sl, :], w1_ref[...], preferred_element_type=F32)
        u = jnp.square(jnp.maximum(u, 0.0)).astype(BF16)
        return jnp.dot(u, w2_ref[...], preferred_element_type=F32)

    @pl.when(f == 0)
    def _():
        for c in range(y_ref.shape[1] // rows):
            sl = pl.ds(c * rows, rows)
            h2_ref[sl, :] = _modulated_norm(x1_ref[0, sl, :], gmlp_ref[...], sc2_ref[0], sh2_ref[0])
            y_ref[0, sl, :] = contribution(sl)

    @pl.when((f != 0) & (f != last))
    def _():
        y_ref[0] += contribution(slice(None))

    @pl.when(f == last)
    def _():
        for c in range(y_ref.shape[1] // rows):
            sl = pl.ds(c * rows, rows)
            x2 = x1_ref[0, sl, :] + gt2_ref[0] * (y_ref[0, sl, :] + contribution(sl))
            ms = jnp.mean(x2 * x2, axis=-1, keepdims=True)
            y_ref[0, sl, :] = x2 * lax.rsqrt(ms + EPS) * gf_ref[...]


def _mlp(x1, w1, w2, ada3, g_mlp, g_final, tm=512, tf=1024):
    b, s, _ = x1.shape

    def ada_spec(idx):
        return pl.BlockSpec((1, 1, D_MODEL), lambda bi, i, f: (bi * 6 + idx, 0, 0))

    return pl.pallas_call(
        functools.partial(_mlp_kernel, rows=tm // 2),
        out_shape=jax.ShapeDtypeStruct((b, s, D_MODEL), F32),
        grid=(b, s // tm, D_FF // tf),
        in_specs=[
            pl.BlockSpec((D_MODEL, tf), lambda bi, i, f: (0, f)),
            pl.BlockSpec((tf, D_MODEL), lambda bi, i, f: (f, 0)),
            pl.BlockSpec((1, tm, D_MODEL), lambda bi, i, f: (bi, i, 0)),
            pl.BlockSpec((1, D_MODEL), lambda bi, i, f: (0, 0)),
            ada_spec(4),
            ada_spec(3),
            ada_spec(5),
            pl.BlockSpec((1, D_MODEL), lambda bi, i, f: (0, 0)),
        ],
        out_specs=pl.BlockSpec((1, tm, D_MODEL), lambda bi, i, f: (bi, i, 0)),
        scratch_shapes=[pltpu.VMEM((tm, D_MODEL), BF16)],
        compiler_params=_params(("parallel", "parallel", "arbitrary")),
        name="mlp_final_norm",
    )(w1, w2, x1, g_mlp, ada3, ada3, ada3, g_final)


def _trunk(x, ada, w, table, lam4):
    b = x.shape[0]
    ada3 = ada.reshape(b * 6, 1, D_MODEL)
    proj = _inproj(x, ada3, w["g_mix"], w["w_in"])
    oa = _diff_attention(proj, lam4, w["g_subln"])
    on = _neighborhood_attention(proj, table)
    x1 = _merge(x, oa, on, proj, w["w_pa"], w["w_pb"], w["w_out"], ada3)
    return _mlp(x1, w["w1"], w["w2"], ada3, w["g_mlp"], w["g_final"])


def kernel(x_prompt, x_sample, c_prompt, c_sample, w_ada, b_ada, g_mix, w_in, lam_q1, lam_k1, lam_q2, lam_k2, g_subln, rpb, w_pa, w_pb, w_out, g_mlp, w1, w2, g_final):
    w = {
        "g_mix": g_mix[0].reshape(1, D_MODEL),
        "w_in": w_in[0].astype(BF16),
        "g_subln": g_subln[0].reshape(1, DA_VDIM),
        "w_pa": w_pa[0].astype(BF16),
        "w_pb": w_pb[0].astype(BF16),
        "w_out": w_out[0].astype(BF16),
        "g_mlp": g_mlp[0].reshape(1, D_MODEL),
        "w1": w1[0].astype(BF16),
        "w2": w2[0].astype(BF16),
        "g_final": g_final.reshape(1, D_MODEL),
    }
    nb_p = c_prompt.shape[0]
    ada = _ada(jnp.concatenate([c_prompt, c_sample], axis=0), w_ada[0], b_ada[0])
    table = _na_bias(rpb[0].reshape(-1))
    lam4 = jnp.stack([lam_q1[0], lam_k1[0], lam_q2[0], lam_k2[0]], axis=0)
    y_prompt = _trunk(x_prompt, ada[:nb_p], w, table, lam4)
    y_sample = _trunk(x_sample, ada[nb_p:], w, table, lam4)
    return (y_prompt, y_sample)
```
